```python
import math
import jax, jax.numpy as jnp
from jax import lax
import numpy as np

D_MODEL = 1024
BATCH = 2
SEQ = 8192
DEPTH = 1
DEC_BATCH = 32
DEC_SEQ = 8
PAST_LEN = 8192
PAGE_SIZE = 128

H_M = 4
DH_M = D_MODEL // 2 // H_M
W_M = H_M * DH_M
H_S = 8
DH_S = D_MODEL // 2 // H_S
W_S = H_S * DH_S
N_IN = 4 * W_M + 2 * H_M + 3 * W_S
MLSTM_CHUNK = 64
SB_QBLOCK = 128
SB_BIAS_INIT = -7.0
GATE_SOFTCAP = 15.0
PEER_HEADS = 8
PEER_NKEYS = 128
PEER_NEXPERTS = PEER_NKEYS * PEER_NKEYS
PEER_DKEY = 256
PEER_TOPK = 16
PEER_BLOCK = 128
EPS = 1e-6

kernel_name = 'hymba_mlstm_stickbreak_peer_step'


def rmsnorm(x, g):
    x = x.astype(jnp.float32)
    return x * lax.rsqrt(jnp.mean(x * x, axis=-1, keepdims=True) + EPS) * g


def softcap(a):
    return GATE_SOFTCAP * jnp.tanh(a / GATE_SOFTCAP)


def split_proj(proj):
    sizes = (W_M, W_M, W_M, W_M, H_M, H_M, W_S, W_S, W_S)
    idx, acc = [], 0
    for s in sizes[:-1]:
        acc += s
        idx.append(acc)
    return jnp.split(proj, idx, axis=-1)


def mlstm_chunkwise(q, k, v, i_pre, logf, C0, n0, m0):
    Bn, Hn, T, d = q.shape
    L = math.gcd(T, MLSTM_CHUNK)
    nc = T // L

    def chunks(a):
        return jnp.moveaxis(a.reshape(a.shape[:2] + (nc, L) + a.shape[3:]), 2, 0)

    causal = jnp.tril(jnp.ones((L, L), dtype=bool))

    def step(carry, inp):
        C, n, m = carry
        qc, kc, vc, ic, fc = inp
        b = jnp.cumsum(fc, axis=-1)
        D = jnp.where(causal, b[..., :, None] - b[..., None, :] + ic[..., None, :], -jnp.inf)
        inter = b + m[..., None]
        mt = jnp.maximum(inter, jnp.max(D, axis=-1))
        w_intra = jnp.exp(D - mt[..., None])
        w_inter = jnp.exp(inter - mt)
        S = jnp.einsum('bhtd,bhsd->bhts', qc, kc) * w_intra
        num = w_inter[..., None] * jnp.einsum('bhvk,bhtk->bhtv', C, qc) + jnp.einsum('bhts,bhsv->bhtv', S, vc)
        dot = w_inter * jnp.einsum('bhk,bhtk->bht', n, qc) + jnp.sum(S, axis=-1)
        h = num / jnp.maximum(jnp.abs(dot), jnp.exp(-mt))[..., None]
        mL = mt[..., -1]
        decay = jnp.exp(b[..., -1:] - b + ic - mL[..., None])
        carry_scale = jnp.exp(b[..., -1] + m - mL)
        C_new = carry_scale[..., None, None] * C + jnp.einsum('bhs,bhsv,bhsk->bhvk', decay, vc, kc)
        n_new = carry_scale[..., None] * n + jnp.einsum('bhs,bhsk->bhk', decay, kc)
        return (C_new, n_new, mL), h

    (C, n, m), hs = lax.scan(step, (C0, n0, m0),
                             (chunks(q), chunks(k), chunks(v), chunks(i_pre), chunks(logf)))
    h = jnp.moveaxis(hs, 0, 2).reshape(Bn, Hn, T, d)
    return h, C, n, m


def stick_breaking(q, k, v, sb_bias, q_offset):
    Bn, Tq, Hn, d = q.shape
    Tk = k.shape[1]
    QB = math.gcd(Tq, SB_QBLOCK)
    nb = Tq // QB
    scale = d ** -0.5
    key_pos = jnp.arange(Tk)
    qb = jnp.moveaxis(q.reshape(Bn, nb, QB, Hn, d), 1, 0)
    bias = sb_bias.astype(jnp.float32)[None, :, None, None]

    def block(args):
        qblk, start = args
        q_pos = q_offset + start + jnp.arange(QB)
        visible = key_pos[None, :] < q_pos[:, None]
        z = jnp.einsum('bqhd,bkhd->bhqk', qblk, k) * scale + bias
        log_1m = jnp.where(visible, jax.nn.log_sigmoid(-z), 0.0)
        between = lax.cumsum(log_1m, axis=3, reverse=True) - log_1m
        A = jnp.where(visible, jnp.exp(jax.nn.log_sigmoid(z) + between), 0.0)
        return jnp.einsum('bhqk,bkhd->bqhd', A, v)

    out = lax.map(block, (qb, jnp.arange(nb) * QB))
    return jnp.moveaxis(out, 0, 1).reshape(Bn, Tq, Hn, d)


def peer(xf, w_query, sub_keys_a, sub_keys_b, expert_u, expert_v):
    T = xf.shape[0]
    qry = (xf @ w_query).reshape(T, PEER_HEADS, 2, PEER_DKEY // 2)
    sa = jnp.einsum('thd,nd->thn', qry[:, :, 0], sub_keys_a)
    sb = jnp.einsum('thd,nd->thn', qry[:, :, 1], sub_keys_b)
    va, ia = lax.top_k(sa, PEER_TOPK)
    vb, ib = lax.top_k(sb, PEER_TOPK)
    cand = (va[..., :, None] + vb[..., None, :]).reshape(T, PEER_HEADS, PEER_TOPK * PEER_TOPK)
    cand_idx = (ia[..., :, None] * PEER_NKEYS + ib[..., None, :]).reshape(T, PEER_HEADS, PEER_TOPK * PEER_TOPK)
    top_s, pos = lax.top_k(cand, PEER_TOPK)
    experts = jnp.take_along_axis(cand_idx, pos, axis=-1).reshape(T, PEER_HEADS * PEER_TOPK)
    gates = jax.nn.softmax(top_s.astype(jnp.float32), axis=-1).reshape(T, PEER_HEADS * PEER_TOPK)
    PB = math.gcd(T, PEER_BLOCK)
    nb = T // PB

    def block(args):
        xb, eb, gb = args
        act = jax.nn.gelu(jnp.einsum('tkd,td->tk', expert_u[eb], xb), approximate=False)
        return jnp.einsum('tk,tkd->td', gb * act, expert_v[eb])

    out = lax.map(block, (xf.reshape(nb, PB, D_MODEL), experts.reshape(nb, PB, -1), gates.reshape(nb, PB, -1)))
    return out.reshape(T, D_MODEL)


def decoder_layer(x, c, k_past, v_past, C0, n0, m0, q_offset, w_ada, b_ada, g_mix, g_ffn, w_in, b_gates,
                  g_q, g_k, sb_bias, g_head, w_out, w_query, sub_keys_a, sub_keys_b, expert_u, expert_v):
    Bn, T, _ = x.shape
    x = x.astype(jnp.float32)
    mod = jax.nn.silu(c.astype(jnp.float32)) @ w_ada + b_ada
    shift_a, scale_a, gate_a, shift_f, scale_f, gate_f = jnp.split(mod[:, None, :], 6, axis=-1)

    h = rmsnorm(x, g_mix) * (1.0 + scale_a) + shift_a
    mq, mk, mv, mo, mi, mf, sq, sk, sv = split_proj(h @ w_in)

    def heads_m(a):
        return a.reshape(Bn, T, H_M, DH_M).transpose(0, 2, 1, 3)

    i_pre = softcap(mi + b_gates[:H_M]).transpose(0, 2, 1)
    logf = jax.nn.log_sigmoid(softcap(mf + b_gates[H_M:])).transpose(0, 2, 1)
    hm, C, n, m = mlstm_chunkwise(heads_m(mq), heads_m(mk) * DH_M ** -0.5, heads_m(mv), i_pre, logf,
                                  C0.astype(jnp.float32), n0.astype(jnp.float32), m0.astype(jnp.float32))
    hm = rmsnorm(hm, g_head[:, None, :]).transpose(0, 2, 1, 3).reshape(Bn, T, W_M) * jax.nn.sigmoid(mo)

    q_s = rmsnorm(sq.reshape(Bn, T, H_S, DH_S), g_q)
    k_s = rmsnorm(sk.reshape(Bn, T, H_S, DH_S), g_k)
    v_s = sv.reshape(Bn, T, H_S, DH_S)
    if k_past is None:
        k_all, v_all = k_s, v_s
    else:
        k_all = jnp.concatenate([k_past.astype(jnp.float32), k_s], axis=1)
        v_all = jnp.concatenate([v_past.astype(jnp.float32), v_s], axis=1)
    hs = stick_breaking(q_s, k_all, v_all, sb_bias, q_offset).reshape(Bn, T, W_S)

    x = x + gate_a * (jnp.concatenate([hm, hs], axis=-1) @ w_out)

    hf = rmsnorm(x, g_ffn) * (1.0 + scale_f) + shift_f
    x = x + gate_f * peer(hf.reshape(Bn * T, D_MODEL), w_query, sub_keys_a, sub_keys_b,
                          expert_u, expert_v).reshape(Bn, T, D_MODEL)
    return x, k_s, v_s, C, n, m


def setup_inputs(seed: int = 0) -> dict:
    key = jax.random.key(seed)
    ks = jax.random.split(key, 32)
    n_pages = PAST_LEN // PAGE_SIZE
    n_pool = (DEC_BATCH * n_pages * 5 + 3) // 4
    nrm = jax.random.normal
    f32 = jnp.float32
    page_table = jax.random.permutation(ks[0], n_pool)[:DEC_BATCH * n_pages].reshape(DEC_BATCH, n_pages).astype(jnp.int32)
    f_bias = jnp.linspace(3.0, 6.0, H_M)[None, :] + 0.1 * nrm(ks[1], (DEPTH, H_M), f32)
    i_bias = 0.1 * nrm(ks[2], (DEPTH, H_M), f32)
    return {
        'x_prompt': nrm(ks[3], (BATCH, SEQ, D_MODEL), f32),
        'x_sample': nrm(ks[4], (DEC_BATCH, DEC_SEQ, D_MODEL), f32),
        'cache_sb_k': nrm(ks[5], (DEPTH, n_pool, PAGE_SIZE, H_S, DH_S), f32),
        'cache_sb_v': nrm(ks[6], (DEPTH, n_pool, PAGE_SIZE, H_S, DH_S), f32),
        'state_mlstm_C': 0.3 * nrm(ks[7], (DEPTH, DEC_BATCH, H_M, DH_M, DH_M), f32),
        'state_mlstm_n': 0.3 * nrm(ks[8], (DEPTH, DEC_BATCH, H_M, DH_M), f32),
        'state_mlstm_m': nrm(ks[9], (DEPTH, DEC_BATCH, H_M), f32),
        'page_table': page_table,
        'c_prompt': nrm(ks[10], (BATCH, D_MODEL), f32),
        'c_sample': nrm(ks[11], (DEC_BATCH, D_MODEL), f32),
        'w_ada': nrm(ks[12], (DEPTH, D_MODEL, 6 * D_MODEL), f32) * D_MODEL ** -0.5,
        'b_ada': 0.02 * nrm(ks[13], (DEPTH, 6 * D_MODEL), f32),
        'norm_mix_g': 1.0 + 0.02 * nrm(ks[14], (DEPTH, D_MODEL), f32),
        'norm_ffn_g': 1.0 + 0.02 * nrm(ks[15], (DEPTH, D_MODEL), f32),
        'w_in': nrm(ks[16], (DEPTH, D_MODEL, N_IN), f32) * D_MODEL ** -0.5,
        'b_gates': jnp.concatenate([i_bias, f_bias], axis=-1),
        'qnorm_g': 1.0 + 0.02 * nrm(ks[17], (DEPTH, H_S, DH_S), f32),
        'knorm_g': 1.0 + 0.02 * nrm(ks[18], (DEPTH, H_S, DH_S), f32),
        'sb_bias': SB_BIAS_INIT + 0.3 * nrm(ks[26], (DEPTH, H_S), f32),
        'mlstm_head_g': 1.0 + 0.02 * nrm(ks[19], (DEPTH, H_M, DH_M), f32),
        'w_out': nrm(ks[20], (DEPTH, D_MODEL, D_MODEL), f32) * D_MODEL ** -0.5,
        'w_query': nrm(ks[21], (DEPTH, D_MODEL, PEER_HEADS * PEER_DKEY), f32) * D_MODEL ** -0.5,
        'sub_keys_a': nrm(ks[22], (DEPTH, PEER_NKEYS, PEER_DKEY // 2), f32) * (PEER_DKEY // 2) ** -0.5,
        'sub_keys_b': nrm(ks[23], (DEPTH, PEER_NKEYS, PEER_DKEY // 2), f32) * (PEER_DKEY // 2) ** -0.5,
        'expert_u': nrm(ks[24], (DEPTH, PEER_NEXPERTS, D_MODEL), f32) * D_MODEL ** -0.5,
        'expert_v': 0.5 * nrm(ks[25], (DEPTH, PEER_NEXPERTS, D_MODEL), f32),
    }


def reference(x_prompt, x_sample, cache_sb_k, cache_sb_v, state_mlstm_C, state_mlstm_n, state_mlstm_m,
              page_table, c_prompt, c_sample, w_ada, b_ada, norm_mix_g, norm_ffn_g, w_in, b_gates,
              qnorm_g, knorm_g, sb_bias, mlstm_head_g, w_out, w_query, sub_keys_a, sub_keys_b, expert_u, expert_v):
    bp = x_prompt.shape[0]
    bs = x_sample.shape[0]
    past_len = page_table.shape[1] * cache_sb_k.shape[2]
    yp, ys = x_prompt, x_sample
    kp_l, vp_l, Cp_l, np_l, mp_l = [], [], [], [], []
    ks_l, vs_l, Cs_l, ns_l, ms_l = [], [], [], [], []
    for l in range(DEPTH):
        lw = (w_ada[l], b_ada[l], norm_mix_g[l], norm_ffn_g[l], w_in[l], b_gates[l], qnorm_g[l],
              knorm_g[l], sb_bias[l], mlstm_head_g[l], w_out[l], w_query[l], sub_keys_a[l], sub_keys_b[l],
              expert_u[l], expert_v[l])
        C0 = jnp.zeros((bp, H_M, DH_M, DH_M), jnp.float32)
        n0 = jnp.zeros((bp, H_M, DH_M), jnp.float32)
        m0 = jnp.zeros((bp, H_M), jnp.float32)
        yp, kp, vp, Cp, np_, mp = decoder_layer(yp, c_prompt, None, None, C0, n0, m0, 0, *lw)
        k_past = cache_sb_k[l][page_table].reshape(bs, past_len, H_S, DH_S)
        v_past = cache_sb_v[l][page_table].reshape(bs, past_len, H_S, DH_S)
        ys, ks_, vs_, Cs, ns, ms = decoder_layer(ys, c_sample, k_past, v_past, state_mlstm_C[l],
                                                 state_mlstm_n[l], state_mlstm_m[l], past_len, *lw)
        kp_l.append(kp.astype(cache_sb_k.dtype)); vp_l.append(vp.astype(cache_sb_v.dtype))
        Cp_l.append(Cp); np_l.append(np_); mp_l.append(mp)
        ks_l.append(ks_.astype(cache_sb_k.dtype)); vs_l.append(vs_.astype(cache_sb_v.dtype))
        Cs_l.append(Cs); ns_l.append(ns); ms_l.append(ms)
    y_prompt = yp.astype(x_prompt.dtype)
    y_sample = ys.astype(x_sample.dtype)
    k_prompt = jnp.stack(kp_l)
    v_prompt = jnp.stack(vp_l)
    C_prompt = jnp.stack(Cp_l)
    n_prompt = jnp.stack(np_l)
    m_prompt = jnp.stack(mp_l)
    k_sample = jnp.stack(ks_l)
    v_sample = jnp.stack(vs_l)
    C_sample = jnp.stack(Cs_l)
    n_sample = jnp.stack(ns_l)
    m_sample = jnp.stack(ms_l)
    return (y_prompt, y_sample, k_prompt, v_prompt, C_prompt, n_prompt, m_prompt,
            k_sample, v_sample, C_sample, n_sample, m_sample)
```

```python
import functools

import numpy as np
import jax
import jax.numpy as jnp
from jax import lax
from jax.experimental import pallas as pl
from jax.experimental.pallas import tpu as pltpu

F32 = jnp.float32
BF16 = jnp.bfloat16

EPS = 1e-6
GATE_SOFTCAP = 15.0
SB_HEADS = 8
SB_DH = 64
ML_HEADS = 4
ML_DH = 128
PEER_HEADS = 8
PEER_NKEYS = 128
PEER_TOPK = 16

LANES = 128
VMEM_LIMIT = 56 * 1024 * 1024

NEG_INF = float("-inf")


def _nn(a, b):
    return jnp.dot(a, b, preferred_element_type=F32)


def _nt(a, b):
    return lax.dot_general(a, b, (((1,), (1,)), ((), ())), preferred_element_type=F32)


def _split(x):
    hi = x.astype(BF16)
    lo = (x - hi.astype(F32)).astype(BF16)
    return hi, lo


def _softplus(z):
    return jnp.maximum(z, 0.0) + jnp.log1p(jnp.exp(-jnp.abs(z)))


def _log_sigmoid(z):
    return -_softplus(-z)


def _softcap(a):
    return GATE_SOFTCAP * jnp.tanh(a / GATE_SOFTCAP)


def _params(*sem):
    return pltpu.CompilerParams(dimension_semantics=sem, vmem_limit_bytes=VMEM_LIMIT)


def _ada_kernel(c_ref, w_ref, b_ref, o_ref):
    c = c_ref[...]
    a = (c * jax.nn.sigmoid(c)).astype(BF16)
    o_ref[...] = _nn(a, w_ref[...]) + b_ref[...]


def _ada(c_all, w_ada, b_ada):
    rows, d = c_all.shape
    n_out = w_ada.shape[1]
    return pl.pallas_call(
        _ada_kernel,
        out_shape=jax.ShapeDtypeStruct((rows, n_out), F32),
        grid=(n_out // d,),
        in_specs=[pl.BlockSpec((rows, d), lambda j: (0, 0)),
                  pl.BlockSpec((d, d), lambda j: (0, j)),
                  pl.BlockSpec((1, d), lambda j: (0, j))],
        out_specs=pl.BlockSpec((rows, d), lambda j: (0, j)),
        compiler_params=_params("parallel"),
        name="ada_mod",
    )(c_all, w_ada, b_ada)


def _in_kernel(x_ref, sc_ref, sh_ref, g_ref, w_ref, wg_ref, wgt_ref, bgr_ref, bgc_ref, gq_ref, gk_ref,
               bd_ref, mq_ref, mk_ref, mv_ref, og_ref, gcol_ref, grow_ref, qn_ref, kbf_ref, ks_ref,
               vs_ref, vt_ref):
    w_m = ML_HEADS * ML_DH
    x = x_ref[...]
    h = x * lax.rsqrt(jnp.mean(x * x, axis=-1, keepdims=True) + EPS) * g_ref[...]
    h = h * (1.0 + sc_ref[...]) + sh_ref[...]
    hb = h.astype(BF16)

    def seg(k):
        return _nn(hb, w_ref[:, k * w_m:(k + 1) * w_m])

    mq_ref[...] = seg(0).astype(BF16)
    mk_ref[...] = (seg(1) * ML_DH ** -0.5).astype(BF16)
    mv_ref[...] = seg(2).astype(BF16)
    og_ref[...] = jax.nn.sigmoid(seg(3)).astype(BF16)

    gc = _softcap(_nn(hb, wg_ref[...]) + bgr_ref[...])
    lane = lax.broadcasted_iota(jnp.int32, gc.shape, 1)
    gcol_ref[...] = jnp.where(lane < ML_HEADS, gc, _log_sigmoid(gc))
    gr = _softcap(_nt(wgt_ref[...], hb) + bgc_ref[...])
    row = lax.broadcasted_iota(jnp.int32, gr.shape, 0)
    grow_ref[...] = jnp.where(row < ML_HEADS, gr, _log_sigmoid(gr))

    def head_norm(s, g):
        hi, lo = _split(s * s)
        ss = _nn(hi, bd_ref[...]) + _nn(lo, bd_ref[...])
        return s * lax.rsqrt(ss * (1.0 / SB_DH) + EPS) * g

    qn_ref[...] = (head_norm(seg(4), gq_ref[...]) * SB_DH ** -0.5).astype(BF16)
    kn = head_norm(seg(5), gk_ref[...])
    ks_ref[...] = kn
    kbf_ref[...] = kn.astype(BF16)
    sv = seg(6)
    vs_ref[...] = sv
    for p in range(w_m // LANES):
        vt_ref[p * LANES:(p + 1) * LANES, :] = sv[:, p * LANES:(p + 1) * LANES].T.astype(BF16)


def _in_proj(x2d, sc3, sh3, g_mix, w_main, w_gate, w_gate_t, bg_row, bg_col, gq, gk, bd, *, tm,
             tiles_per_mod):
    n, d = x2d.shape
    nt = n // tm
    r = sc3.shape[1]
    w = ML_HEADS * ML_DH
    const = lambda t: (0, 0)
    mod_map = lambda t: (t // tiles_per_mod, 0, 0)
    tile = lambda t: (t, 0)
    tok = lambda dt: jax.ShapeDtypeStruct((n, w), dt)
    return pl.pallas_call(
        _in_kernel,
        out_shape=(tok(BF16), tok(BF16), tok(BF16), tok(BF16),
                   jax.ShapeDtypeStruct((n, LANES), F32), jax.ShapeDtypeStruct((8, n), F32),
                   tok(BF16), tok(BF16), tok(F32), tok(F32),
                   jax.ShapeDtypeStruct((nt, w, tm), BF16)),
        grid=(nt,),
        in_specs=[pl.BlockSpec((tm, d), tile),
                  pl.BlockSpec((None, r, d), mod_map),
                  pl.BlockSpec((None, r, d), mod_map),
                  pl.BlockSpec((1, d), const),
                  pl.BlockSpec(w_main.shape, const),
                  pl.BlockSpec(w_gate.shape, const),
                  pl.BlockSpec(w_gate_t.shape, const),
                  pl.BlockSpec(bg_row.shape, const),
                  pl.BlockSpec(bg_col.shape, const),
                  pl.BlockSpec((1, w), const),
                  pl.BlockSpec((1, w), const),
                  pl.BlockSpec((w, w), const)],
        out_specs=(pl.BlockSpec((tm, w), tile), pl.BlockSpec((tm, w), tile), pl.BlockSpec((tm, w), tile),
                   pl.BlockSpec((tm, w), tile), pl.BlockSpec((tm, LANES), tile),
                   pl.BlockSpec((8, tm), lambda t: (0, t)),
                   pl.BlockSpec((tm, w), tile), pl.BlockSpec((tm, w), tile), pl.BlockSpec((tm, w), tile),
                   pl.BlockSpec((tm, w), tile), pl.BlockSpec((None, w, tm), lambda t: (t, 0, 0))),
        compiler_params=_params("parallel"),
        name="in_proj",
    )(x2d, sc3, sh3, g_mix, w_main, w_gate, w_gate_t, bg_row, bg_col, gq, gk, bd)


def _mlstm_kernel(q_ref, k_ref, v_ref, og_ref, gcol_ref, grow_ref, c0_ref, n0_ref, m0_ref, gh_ref,
                  hm_ref, c_ref, n_ref, m_ref, *, chunk):
    L = chunk

    @pl.when(pl.program_id(1) == 0)
    def _():
        c_ref[...] = c0_ref[...]
        n_ref[...] = n0_ref[...]
        m_ref[...] = m0_ref[...]

    t_i = lax.broadcasted_iota(jnp.int32, (L, L), 0)
    s_i = lax.broadcasted_iota(jnp.int32, (L, L), 1)
    causal = s_i <= t_i
    tri_lo = causal.astype(BF16)
    tri_up = (t_i <= s_i).astype(BF16)

    gcol = gcol_ref[...]
    grow = grow_ref[...]
    chi, clo = _split(gcol)
    bc = _nn(tri_lo, chi) + _nn(tri_lo, clo)
    rhi, rlo = _split(grow)
    br = _nn(rhi, tri_up) + _nn(rlo, tri_up)

    for h in range(ML_HEADS):
        sl = slice(h * ML_DH, (h + 1) * ML_DH)
        icol = gcol[:, h:h + 1]
        bcol = bc[:, ML_HEADS + h:ML_HEADS + h + 1]
        irow = grow[h:h + 1, :]
        brow = br[ML_HEADS + h:ML_HEADS + h + 1, :]
        m_prev = m_ref[h:h + 1, 0:1]

        dmat = jnp.where(causal, bcol - brow + irow, NEG_INF)
        inter = bcol + m_prev
        mt = jnp.maximum(inter, jnp.max(dmat, axis=1, keepdims=True))
        w_intra = jnp.exp(dmat - mt)
        w_inter = jnp.exp(inter - mt)

        q = q_ref[:, sl]
        k = k_ref[:, sl]
        v = v_ref[:, sl]
        sw = _nt(q, k) * w_intra
        n_row = n_ref[h:h + 1, :]
        num = w_inter * _nt(q, c_ref[h].astype(BF16)) + _nn(sw.astype(BF16), v)
        qn = jnp.sum(q.astype(F32) * n_row, axis=1, keepdims=True)
        dot = w_inter * qn + jnp.sum(sw, axis=1, keepdims=True)
        hh = num / jnp.maximum(jnp.abs(dot), jnp.exp(-mt))
        hn = hh * lax.rsqrt(jnp.mean(hh * hh, axis=1, keepdims=True) + EPS) * gh_ref[:, sl]
        hm_ref[:, sl] = (hn * og_ref[:, sl].astype(F32)).astype(BF16)

        m_last = mt[L - 1:L, :]
        b_last = bcol[L - 1:L, :]
        decay = jnp.exp(b_last - bcol + icol - m_last)
        carry = jnp.exp(b_last + m_prev - m_last)
        vd = (v.astype(F32) * decay).T.astype(BF16)
        c_ref[h] = carry * c_ref[h] + _nn(vd, k)
        n_ref[h:h + 1, :] = carry * n_row + jnp.sum(decay * k.astype(F32), axis=0, keepdims=True)
        m_ref[h:h + 1, :] = jnp.broadcast_to(m_last, (1, ML_DH))


def _mlstm(mq, mk, mv, og, gcol, grow, c0, n0, m0, gh, *, nb, chunk):
    n, w = mq.shape
    nc = n // (nb * chunk)
    blk = lambda b, c: (b * nc + c, 0)
    st4 = lambda b, c: (b, 0, 0, 0)
    st3 = lambda b, c: (b, 0, 0)
    return pl.pallas_call(
        functools.partial(_mlstm_kernel, chunk=chunk),
        out_shape=(jax.ShapeDtypeStruct((n, w), BF16),
                   jax.ShapeDtypeStruct((nb, ML_HEADS, ML_DH, ML_DH), F32),
                   jax.ShapeDtypeStruct((nb, ML_HEADS, ML_DH), F32),
                   jax.ShapeDtypeStruct((nb, ML_HEADS, ML_DH), F32)),
        grid=(nb, nc),
        in_specs=[pl.BlockSpec((chunk, w), blk), pl.BlockSpec((chunk, w), blk),
                  pl.BlockSpec((chunk, w), blk), pl.BlockSpec((chunk, w), blk),
                  pl.BlockSpec((chunk, LANES), blk),
                  pl.BlockSpec((8, chunk), lambda b, c: (0, b * nc + c)),
                  pl.BlockSpec((None, ML_HEADS, ML_DH, ML_DH), st4),
                  pl.BlockSpec((None, ML_HEADS, ML_DH), st3),
                  pl.BlockSpec((None, ML_HEADS, ML_DH), st3),
                  pl.BlockSpec((1, w), lambda b, c: (0, 0))],
        out_specs=(pl.BlockSpec((chunk, w), blk),
                   pl.BlockSpec((None, ML_HEADS, ML_DH, ML_DH), st4),
                   pl.BlockSpec((None, ML_HEADS, ML_DH), st3),
                   pl.BlockSpec((None, ML_HEADS, ML_DH), st3)),
        compiler_params=_params("parallel", "arbitrary"),
        name="mlstm",
    )(mq, mk, mv, og, gcol, grow, c0, n0, m0, gh)


def _sb_kernel(bias_ref, q_ref, k_ref, vt_ref, o_ref, *, blk):
    hp = pl.program_id(1)
    qi = pl.program_id(2)
    q2 = q_ref[...]
    lane = lax.broadcasted_iota(jnp.int32, q2.shape, 1)
    r_i = lax.broadcasted_iota(jnp.int32, (blk, blk), 0)
    c_i = lax.broadcasted_iota(jnp.int32, (blk, blk), 1)
    suffix = (c_i >= r_i).astype(BF16)
    visible = r_i < c_i

    halves = []
    for hh in range(2):
        qh = jnp.where((lane >= hh * SB_DH) & (lane < (hh + 1) * SB_DH), q2, jnp.zeros_like(q2))
        bias = bias_ref[2 * hp + hh]

        def block(j, carry, masked):
            run, acc = carry
            start = pl.multiple_of(j * blk, blk)
            z = _nt(k_ref[pl.ds(start, blk), :], qh) + bias
            lm = -_softplus(z)
            if masked:
                lm = jnp.where(visible, lm, 0.0)
            hi, lo = _split(lm)
            p = _nn(suffix, hi) + _nn(suffix, lo)
            a = jnp.exp(z + run + p)
            if masked:
                a = jnp.where(visible, a, 0.0)
            acc = acc + _nn(vt_ref[j], a.astype(BF16))
            return run + p[0:1, :], acc

        carry = (jnp.zeros((1, blk), F32), jnp.zeros((LANES, blk), F32))
        carry = block(qi, carry, True)
        _, acc = lax.fori_loop(0, qi, lambda t, c: block(qi - 1 - t, c, False), carry)
        halves.append(acc[hh * SB_DH:(hh + 1) * SB_DH, :])
    o_ref[...] = jnp.concatenate(halves, axis=0).T.astype(BF16)


def _sb_prompt(sb_bias, qn, kbf, vt, *, nb, blk):
    n, w = qn.shape
    t = n // nb
    nq = t // blk
    return pl.pallas_call(
        functools.partial(_sb_kernel, blk=blk),
        out_shape=jax.ShapeDtypeStruct((n, w), BF16),
        grid=(nb, w // LANES, nq),
        in_specs=[pl.BlockSpec(memory_space=pltpu.SMEM),
                  pl.BlockSpec((blk, LANES), lambda b, p, i: (b * nq + i, p)),
                  pl.BlockSpec((t, LANES), lambda b, p, i: (b, p)),
                  pl.BlockSpec((nq, LANES, blk), lambda b, p, i: (b, p, 0))],
        out_specs=pl.BlockSpec((blk, LANES), lambda b, p, i: (b * nq + i, p)),
        compiler_params=_params("parallel", "parallel", "arbitrary"),
        name="sb_prompt",
    )(sb_bias, qn, kbf, vt)


def _sbd_kernel(pt_ref, qbd_ref, bias_ref, knew_ref, vnew_ref, *refs, pages, tq):
    k_refs = refs[:pages]
    v_refs = refs[pages:2 * pages]
    o_ref = refs[2 * pages]
    acc_ref, run_ref = refs[2 * pages + 1:]
    j = pl.program_id(1)
    rows = SB_HEADS * tq
    psz = knew_ref.shape[0]
    qbd = qbd_ref[...]
    bias = bias_ref[...]
    r_i = lax.broadcasted_iota(jnp.int32, (psz, psz), 0)
    c_i = lax.broadcasted_iota(jnp.int32, (psz, psz), 1)
    suffix = (r_i >= c_i).astype(BF16)

    def block(kp, vp, visible):
        z = _nt(qbd, kp) + bias
        lm = -_softplus(z)
        if visible is not None:
            lm = jnp.where(visible, lm, 0.0)
        hi, lo = _split(lm)
        p = _nn(hi, suffix) + _nn(lo, suffix)
        run = run_ref[...]
        a = jnp.exp(z + run + p)
        if visible is not None:
            a = jnp.where(visible, a, 0.0)
        acc_ref[...] += _nn(a.astype(BF16), vp)
        run_ref[...] = run + jnp.broadcast_to(p[:, 0:1], run.shape)

    @pl.when(j == 0)
    def _():
        acc_ref[...] = jnp.zeros_like(acc_ref)
        run_ref[...] = jnp.zeros_like(run_ref)
        q_pos = lax.broadcasted_iota(jnp.int32, (rows, psz), 0) % tq
        k_pos = lax.broadcasted_iota(jnp.int32, (rows, psz), 1)
        block(knew_ref[...], vnew_ref[...], k_pos < q_pos)

    for p in range(pages):
        block(k_refs[p][...].astype(BF16), v_refs[p][...].astype(BF16), None)

    @pl.when(j == pl.num_programs(1) - 1)
    def _():
        acc = acc_ref[...]
        lane = lax.broadcasted_iota(jnp.int32, (tq, acc.shape[1]), 1)
        out = jnp.zeros((tq, acc.shape[1]), F32)
        for h in range(SB_HEADS):
            keep = (lane >= h * SB_DH) & (lane < (h + 1) * SB_DH)
            out = out + jnp.where(keep, acc[h * tq:(h + 1) * tq, :], 0.0)
        o_ref[...] = out.astype(BF16)


def _sb_decode(page_table, qbd, bias_rep, knew, vnew, cache_k, cache_v, *, pages, tq):
    nb, npg = page_table.shape
    psz, w = cache_k.shape[1:]
    rows = SB_HEADS * tq
    steps = npg // pages

    def page_map(p):
        return lambda b, j, pt: (pt[b, npg - 1 - (j * pages + p)], 0, 0)

    seq3 = lambda b, j, pt: (b, 0, 0)
    grid_spec = pltpu.PrefetchScalarGridSpec(
        num_scalar_prefetch=1,
        grid=(nb, steps),
        in_specs=[pl.BlockSpec((None, rows, w), seq3),
                  pl.BlockSpec((rows, psz), lambda b, j, pt: (0, 0)),
                  pl.BlockSpec((None, psz, w), seq3),
                  pl.BlockSpec((None, psz, w), seq3)]
                 + [pl.BlockSpec((None, psz, w), page_map(p)) for p in range(pages)]
                 + [pl.BlockSpec((None, psz, w), page_map(p)) for p in range(pages)],
        out_specs=pl.BlockSpec((None, tq, w), seq3),
        scratch_shapes=[pltpu.VMEM((rows, w), F32), pltpu.VMEM((rows, psz), F32)],
    )
    return pl.pallas_call(
        functools.partial(_sbd_kernel, pages=pages, tq=tq),
        out_shape=jax.ShapeDtypeStruct((nb, tq, w), BF16),
        grid_spec=grid_spec,
        compiler_params=_params("parallel", "arbitrary"),
        name="sb_decode",
    )(page_table, qbd, bias_rep, knew, vnew, *([cache_k] * pages), *([cache_v] * pages))


def _out_kernel(x_ref, hm_ref, hs_ref, ga_ref, scf_ref, shf_ref, gf_ref, wo_ref, wq_ref, ka_ref, kb_ref,
                x1_ref, hf_ref, sat_ref, sbt_ref):
    w_m = hm_ref.shape[1]
    mix = _nn(hm_ref[...], wo_ref[0:w_m, :]) + _nn(hs_ref[...], wo_ref[w_m:, :])
    x1 = x_ref[...] + ga_ref[...] * mix
    x1_ref[...] = x1
    hf = x1 * lax.rsqrt(jnp.mean(x1 * x1, axis=-1, keepdims=True) + EPS) * gf_ref[...]
    hf = (hf * (1.0 + scf_ref[...]) + shf_ref[...]).astype(BF16)
    hf_ref[...] = hf
    dk = ka_ref.shape[1]
    for h in range(PEER_HEADS):
        qa = _nn(hf, wq_ref[:, 2 * h * dk:(2 * h + 1) * dk]).astype(BF16)
        qb = _nn(hf, wq_ref[:, (2 * h + 1) * dk:(2 * h + 2) * dk]).astype(BF16)
        sat_ref[h] = _nt(ka_ref[...], qa)
        sbt_ref[h] = _nt(kb_ref[...], qb)


def _out_proj(x2d, hm, hs, ga3, scf3, shf3, g_ffn, w_out, w_query, ka, kb, *, tm, tiles_per_mod):
    n, d = x2d.shape
    nt = n // tm
    r = ga3.shape[1]
    w = hm.shape[1]
    const = lambda t: (0, 0)
    mod_map = lambda t: (t // tiles_per_mod, 0, 0)
    tile = lambda t: (t, 0)
    sc_shape = jax.ShapeDtypeStruct((PEER_HEADS, PEER_NKEYS, n), F32)
    sc_spec = pl.BlockSpec((PEER_HEADS, PEER_NKEYS, tm), lambda t: (0, 0, t))
    return pl.pallas_call(
        _out_kernel,
        out_shape=(jax.ShapeDtypeStruct((n, d), F32), jax.ShapeDtypeStruct((n, d), BF16),
                   sc_shape, sc_shape),
        grid=(nt,),
        in_specs=[pl.BlockSpec((tm, d), tile), pl.BlockSpec((tm, w), tile), pl.BlockSpec((tm, w), tile),
                  pl.BlockSpec((None, r, d), mod_map), pl.BlockSpec((None, r, d), mod_map),
                  pl.BlockSpec((None, r, d), mod_map),
                  pl.BlockSpec((1, d), const), pl.BlockSpec(w_out.shape, const),
                  pl.BlockSpec(w_query.shape, const), pl.BlockSpec(ka.shape, const),
                  pl.BlockSpec(kb.shape, const)],
        out_specs=(pl.BlockSpec((tm, d), tile), pl.BlockSpec((tm, d), tile), sc_spec, sc_spec),
        compiler_params=_params("parallel"),
        name="out_proj",
    )(x2d, hm, hs, ga3, scf3, shf3, g_ffn, w_out, w_query, ka, kb)


_CAND_ROWS = 80


def _cand_tables():
    pos = np.full((_CAND_ROWS, LANES), 1e9, np.float32)
    for j in range(PEER_TOPK):
        pos[j] = j
    for i in range(1, 8):
        for j in range(PEER_TOPK // (i + 1)):
            pos[16 + 8 * (i - 1) + j] = i * PEER_TOPK + j
    for r in range(8):
        pos[72 + r] = (8 + r) * PEER_TOPK
    return pos


def _sel_kernel(pos_ref, sa_ref, sb_ref, cnt_ref, rb_ref, ea_ref, eb_ref, va_ref, vb_ref, *, sub):
    nk = PEER_NKEYS
    iota = lax.broadcasted_iota(jnp.int32, (nk, LANES), 0).astype(F32)
    pos = pos_ref[...]
    valid = pos < 1e8

    def extract(x, v_ref):
        def body(r, carry):
            rem, rank = carry
            m = jnp.max(rem, axis=0, keepdims=True)
            idx = jnp.min(jnp.where(rem == m, iota, float(nk)), axis=0, keepdims=True)
            sel = iota == idx
            v_ref[pl.ds(r, 1), :] = m
            return jnp.where(sel, NEG_INF, rem), jnp.where(sel, r.astype(F32), rank)

        _, rank = lax.fori_loop(0, PEER_TOPK, body, (x, jnp.full_like(x, float(PEER_TOPK))))
        return rank

    for s in range(sub):
        ls = slice(s * LANES, (s + 1) * LANES)
        sa = sa_ref[:, ls]
        sb = sb_ref[:, ls]
        rank_a = extract(sa, va_ref)
        rank_b = extract(sb, vb_ref)
        va = va_ref[...]
        vb = vb_ref[...]
        blocks = [va[0:1, :] + vb]
        for i in range(1, 8):
            blocks.append(va[i:i + 1, :] + vb[0:8, :])
        blocks.append(va[8:16, :] + vb[0:1, :])
        cand = jnp.where(valid, jnp.concatenate(blocks, axis=0), NEG_INF)
        smax = va[0:1, :] + vb[0:1, :]

        def body2(r, carry):
            cand, taken, z = carry
            m = jnp.max(cand, axis=0, keepdims=True)
            p = jnp.min(jnp.where(cand == m, pos, 1e9), axis=0, keepdims=True)
            sel = pos == p
            return (jnp.where(sel, NEG_INF, cand), jnp.where(sel, 1.0, taken), z + jnp.exp(m - smax))

        _, taken, z = lax.fori_loop(
            0, PEER_TOPK, body2, (cand, jnp.zeros_like(cand), jnp.zeros_like(smax)))

        cnt = jnp.zeros_like(sa)
        for i in range(PEER_TOPK):
            if i == 0:
                c = jnp.sum(taken[0:16, :], axis=0, keepdims=True)
            elif i < 8:
                c = jnp.sum(taken[16 + 8 * (i - 1):16 + 8 * i, :], axis=0, keepdims=True)
            else:
                c = taken[72 + (i - 8):73 + (i - 8), :]
            cnt = jnp.where(rank_a == float(i), c, cnt)
        cnt_ref[:, ls] = cnt
        rb_ref[:, ls] = rank_b
        ea_ref[:, ls] = jnp.exp(sa - va[0:1, :]) / z
        eb_ref[:, ls] = jnp.exp(sb - vb[0:1, :])


def _peer_select(pos, sat, sbt, *, ts):
    nh, nk, n = sat.shape
    spec = pl.BlockSpec((None, nk, ts), lambda t, h: (h, 0, t))
    shp = jax.ShapeDtypeStruct((nh, nk, n), F32)
    return pl.pallas_call(
        functools.partial(_sel_kernel, sub=ts // LANES),
        out_shape=(shp, shp, shp, shp),
        grid=(n // ts, nh),
        in_specs=[pl.BlockSpec(pos.shape, lambda t, h: (0, 0)), spec, spec],
        out_specs=(spec, spec, spec, spec),
        scratch_shapes=[pltpu.VMEM((PEER_TOPK, LANES), F32), pltpu.VMEM((PEER_TOPK, LANES), F32)],
        compiler_params=_params("parallel", "parallel"),
        name="peer_select",
    )(pos, sat, sbt)


def _peer_kernel(hf_ref, u_ref, vt_ref, cnt_ref, rb_ref, ea_ref, eb_ref, x1_ref, gf_ref, y_ref, acc_ref,
                 w_ref, *, na):
    e = pl.program_id(1)
    nk = PEER_NKEYS

    @pl.when(e == 0)
    def _():
        acc_ref[...] = jnp.zeros_like(acc_ref)

    act = _nt(u_ref[...], hf_ref[...])
    gel = 0.5 * act * (1.0 + lax.erf(act * (2.0 ** -0.5)))
    for ai in range(na):
        a = e * na + ai
        g = jnp.zeros((nk, act.shape[1]), F32)
        for h in range(PEER_HEADS):
            cnt_row = cnt_ref[h, pl.ds(a, 1), :]
            ea_row = ea_ref[h, pl.ds(a, 1), :]
            g = g + jnp.where(rb_ref[h] < cnt_row, eb_ref[h], 0.0) * ea_row
        w_ref[ai * nk:(ai + 1) * nk, :] = (g * gel[ai * nk:(ai + 1) * nk, :]).astype(BF16)
    acc_ref[...] += _nn(vt_ref[...], w_ref[...])

    @pl.when(e == pl.num_programs(1) - 1)
    def _():
        y_ref[...] = x1_ref[...] + gf_ref[...] * acc_ref[...].T


def _peer_experts(hf, u_bf, vt_bf, cnt, rb, ea, eb, x1, gf3, *, tm, eb_sz, tiles_per_mod):
    n, d = hf.shape
    ne = u_bf.shape[0]
    r = gf3.shape[1]
    na = eb_sz // PEER_NKEYS
    sel = pl.BlockSpec((PEER_HEADS, PEER_NKEYS, tm), lambda t, e: (0, 0, t))
    return pl.pallas_call(
        functools.partial(_peer_kernel, na=na),
        out_shape=jax.ShapeDtypeStruct((n, d), F32),
        grid=(n // tm, ne // eb_sz),
        in_specs=[pl.BlockSpec((tm, d), lambda t, e: (t, 0)),
                  pl.BlockSpec((eb_sz, d), lambda t, e: (e, 0)),
                  pl.BlockSpec((d, eb_sz), lambda t, e: (0, e)),
                  sel, sel, sel, sel,
                  pl.BlockSpec((tm, d), lambda t, e: (t, 0)),
                  pl.BlockSpec((None, r, d), lambda t, e: (t // tiles_per_mod, 0, 0))],
        out_specs=pl.BlockSpec((tm, d), lambda t, e: (t, 0)),
        scratch_shapes=[pltpu.VMEM((d, tm), F32), pltpu.VMEM((eb_sz, tm), BF16)],
        compiler_params=_params("parallel", "arbitrary"),
        name="peer_experts",
    )(hf, u_bf, vt_bf, cnt, rb, ea, eb, x1, gf3)


IN_TM = 256
ML_CHUNK = 256
ML_CHUNK_DEC = 128
SB_DEC_PAGES = 4
SEL_TS = 512
PEER_TM = 512
PEER_EB = 512


def _layer_weights(w_in, b_gates, qnorm_g, knorm_g, w_out, w_query, sub_keys_a, sub_keys_b, expert_u,
                   expert_v):
    w_m = ML_HEADS * ML_DH
    ng = 2 * ML_HEADS
    g0 = 4 * w_m
    d = w_in.shape[0]
    w_main = jnp.concatenate([w_in[:, :g0], w_in[:, g0 + ng:]], axis=1).astype(BF16)
    wg = w_in[:, g0:g0 + ng]
    w_gate = jnp.pad(wg, ((0, 0), (0, LANES - ng))).astype(BF16)
    w_gate_t = wg.T.astype(BF16)
    bg_row = jnp.pad(b_gates, (0, LANES - ng)).reshape(1, LANES)
    bg_col = b_gates.reshape(ng, 1)
    head = np.arange(SB_HEADS * SB_DH) // SB_DH
    bd = jnp.asarray(head[:, None] == head[None, :], BF16)
    return dict(w_main=w_main, w_gate=w_gate, w_gate_t=w_gate_t, bg_row=bg_row, bg_col=bg_col,
                gq=qnorm_g.reshape(1, -1), gk=knorm_g.reshape(1, -1), bd=bd,
                w_out=w_out.astype(BF16), w_query=w_query.astype(BF16),
                ka=sub_keys_a.astype(BF16), kb=sub_keys_b.astype(BF16),
                u=expert_u.astype(BF16), vt=expert_v.T.astype(BF16), d=d)


def kernel(x_prompt, x_sample, cache_sb_k, cache_sb_v, state_mlstm_C, state_mlstm_n, state_mlstm_m,
           page_table, c_prompt, c_sample, w_ada, b_ada, norm_mix_g, norm_ffn_g, w_in, b_gates, qnorm_g,
           knorm_g, sb_bias, mlstm_head_g, w_out, w_query, sub_keys_a, sub_keys_b, expert_u, expert_v):
    depth = w_ada.shape[0]
    assert depth == 1, "single-layer trunk"
    bp, tp, d = x_prompt.shape
    bs, ts, _ = x_sample.shape
    w_m = ML_HEADS * ML_DH
    w_s = SB_HEADS * SB_DH
    psz = cache_sb_k.shape[2]

    lw = _layer_weights(w_in[0], b_gates[0], qnorm_g[0], knorm_g[0], w_out[0], w_query[0], sub_keys_a[0],
                        sub_keys_b[0], expert_u[0], expert_v[0])
    g_mix = norm_mix_g[0].reshape(1, d)
    g_ffn = norm_ffn_g[0].reshape(1, d)
    gh = mlstm_head_g[0].reshape(1, w_m)
    bias = sb_bias[0].astype(F32)
    pos = jnp.asarray(_cand_tables())

    nrow = bp + bs
    pad = (-nrow) % 8
    c_all = jnp.pad(jnp.concatenate([c_prompt, c_sample], axis=0).astype(F32), ((0, pad), (0, 0)))
    mod = _ada(c_all, w_ada[0].astype(BF16), b_ada[0].reshape(1, -1))
    mods = [mod[:, i * d:(i + 1) * d] for i in range(6)]

    def group_mods(lo, hi, per_token):
        out = []
        for m in mods:
            m = m[lo:hi]
            if per_token:
                m = jnp.repeat(m, ts, axis=0).reshape(1, (hi - lo) * ts, d)
            else:
                m = m.reshape(hi - lo, 1, d)
            out.append(m)
        return out

    def peer(hf, x1, gf3, sat, sbt, n_tok, tm, tpm):
        sel_ts = min(SEL_TS, n_tok)
        cnt, rb, ea, eb = _peer_select(pos, sat, sbt, ts=sel_ts)
        return _peer_experts(hf, lw["u"], lw["vt"], cnt, rb, ea, eb, x1, gf3, tm=tm, eb_sz=PEER_EB,
                             tiles_per_mod=tpm)

    sh_a, sc_a, ga, sh_f, sc_f, gf = group_mods(0, bp, False)
    xp = x_prompt.reshape(bp * tp, d).astype(F32)
    tpm = tp // IN_TM
    (mq, mk, mv, og, gcol, grow, qn, kbf, ks_p, vs_p, vt) = _in_proj(
        xp, sc_a, sh_a, g_mix, lw["w_main"], lw["w_gate"], lw["w_gate_t"], lw["bg_row"], lw["bg_col"],
        lw["gq"], lw["gk"], lw["bd"], tm=IN_TM, tiles_per_mod=tpm)
    zc = jnp.zeros((bp, ML_HEADS, ML_DH, ML_DH), F32)
    zn = jnp.zeros((bp, ML_HEADS, ML_DH), F32)
    hm, c_p, n_p, m_p = _mlstm(mq, mk, mv, og, gcol, grow, zc, zn, zn, gh, nb=bp, chunk=ML_CHUNK)
    hs = _sb_prompt(bias, qn, kbf, vt, nb=bp, blk=IN_TM)
    x1, hf, sat, sbt = _out_proj(xp, hm, hs, ga, sc_f, sh_f, g_ffn, lw["w_out"], lw["w_query"], lw["ka"],
                                 lw["kb"], tm=IN_TM, tiles_per_mod=tpm)
    y_p = peer(hf, x1, gf, sat, sbt, bp * tp, PEER_TM, tp // PEER_TM)

    sh_a, sc_a, ga, sh_f, sc_f, gf = group_mods(bp, bp + bs, True)
    ns = bs * ts
    xs = x_sample.reshape(ns, d).astype(F32)
    (mq, mk, mv, og, gcol, grow, qn, kbf, ks_s, vs_s, _) = _in_proj(
        xs, sc_a, sh_a, g_mix, lw["w_main"], lw["w_gate"], lw["w_gate_t"], lw["bg_row"], lw["bg_col"],
        lw["gq"], lw["gk"], lw["bd"], tm=ns, tiles_per_mod=1)

    lc = ML_CHUNK_DEC

    def pad_seq(a, fill=0.0):
        a = a.reshape(bs, ts, a.shape[-1])
        return jnp.pad(a, ((0, 0), (0, lc - ts), (0, 0)), constant_values=fill).reshape(bs * lc, -1)

    gcol_fill = jnp.where(jnp.arange(LANES) < ML_HEADS, -1e30, 0.0).astype(F32)
    gcol_p = gcol.reshape(bs, ts, LANES)
    gcol_p = jnp.concatenate(
        [gcol_p, jnp.broadcast_to(gcol_fill, (bs, lc - ts, LANES))], axis=1).reshape(bs * lc, LANES)
    grow_fill = jnp.where(jnp.arange(8) < ML_HEADS, -1e30, 0.0).astype(F32)
    grow_p = grow.reshape(8, bs, ts)
    grow_p = jnp.concatenate(
        [grow_p, jnp.broadcast_to(grow_fill[:, None, None], (8, bs, lc - ts))], axis=2).reshape(8, bs * lc)
    m0 = jnp.broadcast_to(state_mlstm_m[0].astype(F32)[:, :, None], (bs, ML_HEADS, ML_DH))
    hm_pad, c_s, n_s, m_s = _mlstm(pad_seq(mq), pad_seq(mk), pad_seq(mv), pad_seq(og), gcol_p, grow_p,
                                   state_mlstm_C[0].astype(F32), state_mlstm_n[0].astype(F32), m0, gh,
                                   nb=bs, chunk=lc)
    hm = hm_pad.reshape(bs, lc, w_m)[:, :ts].reshape(ns, w_m)

    lane_head = jnp.arange(w_s) // SB_DH
    q3 = qn.reshape(bs, 1, ts, w_s)
    qbd = jnp.where(lane_head[None, None, None, :] == jnp.arange(SB_HEADS)[None, :, None, None], q3,
                    jnp.zeros_like(q3)).reshape(bs, SB_HEADS * ts, w_s)
    bias_rep = jnp.broadcast_to(jnp.repeat(bias, ts)[:, None], (SB_HEADS * ts, psz))
    knew = jnp.pad(kbf.reshape(bs, ts, w_s), ((0, 0), (0, psz - ts), (0, 0)))
    vnew = jnp.pad(vs_s.astype(BF16).reshape(bs, ts, w_s), ((0, 0), (0, psz - ts), (0, 0)))
    ck = cache_sb_k[0].reshape(cache_sb_k.shape[1], psz, w_s)
    cv = cache_sb_v[0].reshape(cache_sb_v.shape[1], psz, w_s)
    hs = _sb_decode(page_table.astype(jnp.int32), qbd, bias_rep, knew, vnew, ck, cv, pages=SB_DEC_PAGES,
                    tq=ts).reshape(ns, w_s)
    x1, hf, sat, sbt = _out_proj(xs, hm, hs, ga, sc_f, sh_f, g_ffn, lw["w_out"], lw["w_query"], lw["ka"],
                                 lw["kb"], tm=ns, tiles_per_mod=1)
    y_s = peer(hf, x1, gf, sat, sbt, ns, ns, 1)

    dt_k, dt_v = cache_sb_k.dtype, cache_sb_v.dtype
    return (y_p.reshape(bp, tp, d).astype(x_prompt.dtype),
            y_s.reshape(bs, ts, d).astype(x_sample.dtype),
            ks_p.reshape(1, bp, tp, SB_HEADS, SB_DH).astype(dt_k),
            vs_p.reshape(1, bp, tp, SB_HEADS, SB_DH).astype(dt_v),
            c_p[None], n_p[None], m_p[None, :, :, 0],
            ks_s.reshape(1, bs, ts, SB_HEADS, SB_DH).astype(dt_k),
            vs_s.reshape(1, bs, ts, SB_HEADS, SB_DH).astype(dt_v),
            c_s[None], n_s[None], m_s[None, :, :, 0])
```

```python
import functools

import numpy as np
import jax
import jax.numpy as jnp
from jax import lax
from jax.experimental import pallas as pl
from jax.experimental.pallas import tpu as pltpu

F32 = jnp.float32
BF16 = jnp.bfloat16

EPS = 1e-6
GATE_SOFTCAP = 15.0
SB_HEADS = 8
SB_DH = 64
ML_HEADS = 4
ML_DH = 128
PEER_HEADS = 8
PEER_NKEYS = 128
PEER_TOPK = 16

LOG2E = 1.4426950408889634
LANES = 128
_BF16_ROWS = 16
VMEM_LIMIT = 56 * 1024 * 1024

NEG_INF = float("-inf")


def _nn(a, b):
    return jnp.dot(a, b, preferred_element_type=F32)


def _nt(a, b):
    return lax.dot_general(a, b, (((1,), (1,)), ((), ())), preferred_element_type=F32)


def _split(x):
    hi = x.astype(BF16)
    lo = (x - hi.astype(F32)).astype(BF16)
    return hi, lo


def _softplus(z):
    return jnp.maximum(z, 0.0) + jnp.log1p(jnp.exp(-jnp.abs(z)))


def _log_sigmoid(z):
    return -_softplus(-z)


def _softcap(a):
    return GATE_SOFTCAP * jnp.tanh(a / GATE_SOFTCAP)


def _params(*sem):
    return pltpu.CompilerParams(dimension_semantics=sem, vmem_limit_bytes=VMEM_LIMIT)


def _ada_kernel(c_ref, w_ref, b_ref, o_ref):
    c = c_ref[...]
    a = (c * jax.nn.sigmoid(c)).astype(BF16)
    o_ref[...] = _nn(a, w_ref[...]) + b_ref[...]


def _ada(c_all, w_ada, b_ada):
    rows, d = c_all.shape
    n_out = w_ada.shape[1]
    return pl.pallas_call(
        _ada_kernel,
        out_shape=jax.ShapeDtypeStruct((rows, n_out), F32),
        grid=(n_out // d,),
        in_specs=[pl.BlockSpec((rows, d), lambda j: (0, 0)),
                  pl.BlockSpec((d, d), lambda j: (0, j)),
                  pl.BlockSpec((1, d), lambda j: (0, j))],
        out_specs=pl.BlockSpec((rows, d), lambda j: (0, j)),
        compiler_params=_params("parallel"),
        name="ada_mod",
    )(c_all, w_ada, b_ada)


def _in_kernel(x_ref, sc_ref, sh_ref, g_ref, w_ref, wg_ref, wgt_ref, bgr_ref, bgc_ref, gq_ref, gk_ref,
               bd_ref, mq_ref, mk_ref, mv_ref, og_ref, gcol_ref, grow_ref, qn_ref, kbf_ref, ks_ref,
               vs_ref, vt_ref):
    w_m = ML_HEADS * ML_DH
    x = x_ref[...]
    h = x * lax.rsqrt(jnp.mean(x * x, axis=-1, keepdims=True) + EPS) * g_ref[...]
    h = h * (1.0 + sc_ref[...]) + sh_ref[...]
    hb = h.astype(BF16)

    def seg(k):
        return _nn(hb, w_ref[:, k * w_m:(k + 1) * w_m])

    mq_ref[...] = seg(0).astype(BF16)
    mk_ref[...] = (seg(1) * ML_DH ** -0.5).astype(BF16)
    mv_ref[...] = seg(2).astype(BF16)
    og_ref[...] = jax.nn.sigmoid(seg(3)).astype(BF16)

    gc = _softcap(_nn(hb, wg_ref[...]) + bgr_ref[...])
    lane = lax.broadcasted_iota(jnp.int32, gc.shape, 1)
    gcol_ref[...] = jnp.where(lane < ML_HEADS, gc, _log_sigmoid(gc))
    gr = _softcap(_nt(wgt_ref[...], hb) + bgc_ref[...])
    row = lax.broadcasted_iota(jnp.int32, gr.shape, 0)
    grow_ref[...] = jnp.where(row < ML_HEADS, gr, _log_sigmoid(gr))

    def head_norm(s, g):
        hi, lo = _split(s * s)
        ss = _nn(hi, bd_ref[...]) + _nn(lo, bd_ref[...])
        return s * lax.rsqrt(ss * (1.0 / SB_DH) + EPS) * g

    qn_ref[...] = (head_norm(seg(4), gq_ref[...]) * (SB_DH ** -0.5 * LOG2E)).astype(BF16)
    kn = head_norm(seg(5), gk_ref[...])
    ks_ref[...] = kn
    kbf_ref[...] = kn.astype(BF16)
    sv = seg(6)
    vs_ref[...] = sv
    for p in range(w_m // LANES):
        vt_ref[p * LANES:(p + 1) * LANES, :] = sv[:, p * LANES:(p + 1) * LANES].T.astype(BF16)


def _in_proj(x2d, sc3, sh3, g_mix, w_main, w_gate, w_gate_t, bg_row, bg_col, gq, gk, bd, *, tm,
             tiles_per_mod):
    n, d = x2d.shape
    nt = n // tm
    r = sc3.shape[1]
    w = ML_HEADS * ML_DH
    const = lambda t: (0, 0)
    mod_map = lambda t: (t // tiles_per_mod, 0, 0)
    tile = lambda t: (t, 0)
    tok = lambda dt: jax.ShapeDtypeStruct((n, w), dt)
    return pl.pallas_call(
        _in_kernel,
        out_shape=(tok(BF16), tok(BF16), tok(BF16), tok(BF16),
                   jax.ShapeDtypeStruct((n, LANES), F32), jax.ShapeDtypeStruct((8, n), F32),
                   tok(BF16), tok(BF16), tok(F32), tok(F32),
                   jax.ShapeDtypeStruct((nt, w, tm), BF16)),
        grid=(nt,),
        in_specs=[pl.BlockSpec((tm, d), tile),
                  pl.BlockSpec((None, r, d), mod_map),
                  pl.BlockSpec((None, r, d), mod_map),
                  pl.BlockSpec((1, d), const),
                  pl.BlockSpec(w_main.shape, const),
                  pl.BlockSpec(w_gate.shape, const),
                  pl.BlockSpec(w_gate_t.shape, const),
                  pl.BlockSpec(bg_row.shape, const),
                  pl.BlockSpec(bg_col.shape, const),
                  pl.BlockSpec((1, w), const),
                  pl.BlockSpec((1, w), const),
                  pl.BlockSpec((w, w), const)],
        out_specs=(pl.BlockSpec((tm, w), tile), pl.BlockSpec((tm, w), tile), pl.BlockSpec((tm, w), tile),
                   pl.BlockSpec((tm, w), tile), pl.BlockSpec((tm, LANES), tile),
                   pl.BlockSpec((8, tm), lambda t: (0, t)),
                   pl.BlockSpec((tm, w), tile), pl.BlockSpec((tm, w), tile), pl.BlockSpec((tm, w), tile),
                   pl.BlockSpec((tm, w), tile), pl.BlockSpec((None, w, tm), lambda t: (t, 0, 0))),
        compiler_params=_params("parallel"),
        name="in_proj",
    )(x2d, sc3, sh3, g_mix, w_main, w_gate, w_gate_t, bg_row, bg_col, gq, gk, bd)


def _mlstm_kernel(q_ref, k_ref, v_ref, og_ref, gcol_ref, grow_ref, c0_ref, n0_ref, m0_ref, gh_ref,
                  hm_ref, c_ref, n_ref, m_ref, *, chunk):
    L = chunk

    @pl.when(pl.program_id(1) == 0)
    def _():
        c_ref[...] = c0_ref[...]
        n_ref[...] = n0_ref[...]
        m_ref[...] = m0_ref[...]

    t_i = lax.broadcasted_iota(jnp.int32, (L, L), 0)
    s_i = lax.broadcasted_iota(jnp.int32, (L, L), 1)
    causal = s_i <= t_i
    tri_lo = causal.astype(BF16)
    tri_up = (t_i <= s_i).astype(BF16)

    gcol = gcol_ref[...]
    grow = grow_ref[...]
    chi, clo = _split(gcol)
    bc = _nn(tri_lo, chi) + _nn(tri_lo, clo)
    rhi, rlo = _split(grow)
    br = _nn(rhi, tri_up) + _nn(rlo, tri_up)

    for h in range(ML_HEADS):
        sl = slice(h * ML_DH, (h + 1) * ML_DH)
        icol = gcol[:, h:h + 1]
        bcol = bc[:, ML_HEADS + h:ML_HEADS + h + 1]
        irow = grow[h:h + 1, :]
        brow = br[ML_HEADS + h:ML_HEADS + h + 1, :]
        m_prev = m_ref[h:h + 1, 0:1]

        dmat = jnp.where(causal, bcol - brow + irow, NEG_INF)
        inter = bcol + m_prev
        mt = jnp.maximum(inter, jnp.max(dmat, axis=1, keepdims=True))
        w_intra = jnp.exp(dmat - mt)
        w_inter = jnp.exp(inter - mt)

        q = q_ref[:, sl]
        k = k_ref[:, sl]
        v = v_ref[:, sl]
        sw = _nt(q, k) * w_intra
        n_row = n_ref[h:h + 1, :]
        num = w_inter * _nt(q, c_ref[h].astype(BF16)) + _nn(sw.astype(BF16), v)
        qn = jnp.sum(q.astype(F32) * n_row, axis=1, keepdims=True)
        dot = w_inter * qn + jnp.sum(sw, axis=1, keepdims=True)
        hh = num / jnp.maximum(jnp.abs(dot), jnp.exp(-mt))
        hn = hh * lax.rsqrt(jnp.mean(hh * hh, axis=1, keepdims=True) + EPS) * gh_ref[:, sl]
        hm_ref[:, sl] = (hn * og_ref[:, sl].astype(F32)).astype(BF16)

        m_last = mt[L - 1:L, :]
        b_last = bcol[L - 1:L, :]
        decay = jnp.exp(b_last - bcol + icol - m_last)
        carry = jnp.exp(b_last + m_prev - m_last)
        vd = (v.astype(F32) * decay).T.astype(BF16)
        c_ref[h] = carry * c_ref[h] + _nn(vd, k)
        n_ref[h:h + 1, :] = carry * n_row + jnp.sum(decay * k.astype(F32), axis=0, keepdims=True)
        m_ref[h:h + 1, :] = jnp.broadcast_to(m_last, (1, ML_DH))


def _mlstm(mq, mk, mv, og, gcol, grow, c0, n0, m0, gh, *, nb, chunk):
    n, w = mq.shape
    nc = n // (nb * chunk)
    blk = lambda b, c: (b * nc + c, 0)
    st4 = lambda b, c: (b, 0, 0, 0)
    st3 = lambda b, c: (b, 0, 0)
    return pl.pallas_call(
        functools.partial(_mlstm_kernel, chunk=chunk),
        out_shape=(jax.ShapeDtypeStruct((n, w), BF16),
                   jax.ShapeDtypeStruct((nb, ML_HEADS, ML_DH, ML_DH), F32),
                   jax.ShapeDtypeStruct((nb, ML_HEADS, ML_DH), F32),
                   jax.ShapeDtypeStruct((nb, ML_HEADS, ML_DH), F32)),
        grid=(nb, nc),
        in_specs=[pl.BlockSpec((chunk, w), blk), pl.BlockSpec((chunk, w), blk),
                  pl.BlockSpec((chunk, w), blk), pl.BlockSpec((chunk, w), blk),
                  pl.BlockSpec((chunk, LANES), blk),
                  pl.BlockSpec((8, chunk), lambda b, c: (0, b * nc + c)),
                  pl.BlockSpec((None, ML_HEADS, ML_DH, ML_DH), st4),
                  pl.BlockSpec((None, ML_HEADS, ML_DH), st3),
                  pl.BlockSpec((None, ML_HEADS, ML_DH), st3),
                  pl.BlockSpec((1, w), lambda b, c: (0, 0))],
        out_specs=(pl.BlockSpec((chunk, w), blk),
                   pl.BlockSpec((None, ML_HEADS, ML_DH, ML_DH), st4),
                   pl.BlockSpec((None, ML_HEADS, ML_DH), st3),
                   pl.BlockSpec((None, ML_HEADS, ML_DH), st3)),
        compiler_params=_params("parallel", "arbitrary"),
        name="mlstm",
    )(mq, mk, mv, og, gcol, grow, c0, n0, m0, gh)


def _sb_kernel(bias_ref, q_ref, k_ref, vt_ref, o_ref, acc_ref, *, blk, heads):
    grp0 = pl.program_id(1)
    qi = pl.program_id(2)
    lane = lax.broadcasted_iota(jnp.int32, (blk, LANES), 1)
    r_i = lax.broadcasted_iota(jnp.int32, (blk, blk), 0)
    c_i = lax.broadcasted_iota(jnp.int32, (blk, blk), 1)
    suffix = (c_i >= r_i).astype(BF16)
    visible = r_i < c_i

    q_neg, bias_neg = [], []
    for hh in range(heads):
        grp, half = divmod(hh, 2)
        q2 = q_ref[:, grp * LANES:(grp + 1) * LANES]
        keep = (lane >= half * SB_DH) & (lane < (half + 1) * SB_DH)
        q_neg.append(jnp.where(keep, -q2, jnp.zeros_like(q2)))
        bias_neg.append(-LOG2E * bias_ref[heads * grp0 + hh])
    acc_ref[...] = jnp.zeros_like(acc_ref)

    def step(j, runs, masked):
        start = pl.multiple_of(j * blk, blk)
        hs = range(heads)
        gl = [slice((hh // 2) * LANES, (hh // 2 + 1) * LANES) for hh in hs]
        zn = [_nt(k_ref[pl.ds(start, blk), gl[hh]], q_neg[hh]) + bias_neg[hh] for hh in hs]
        lm = [jnp.minimum(z, 0.0) - jnp.log(1.0 + jnp.exp2(-jnp.abs(z))) * LOG2E for z in zn]
        if masked:
            lm = [jnp.where(visible, x, 0.0) for x in lm]
        parts = [_split(x) for x in lm]
        p = [_nn(suffix, hi) + _nn(suffix, lo) for hi, lo in parts]
        a = [jnp.exp2((runs[hh] + p[hh]) - zn[hh]) for hh in hs]
        if masked:
            a = [jnp.where(visible, x, 0.0) for x in a]
        for hh in hs:
            acc_ref[hh] += _nn(vt_ref[j, gl[hh], :], a[hh].astype(BF16))
        return tuple(runs[hh] + p[hh][0:1, :] for hh in hs)

    runs = step(qi, tuple(jnp.zeros((1, blk), F32) for _ in range(heads)), True)
    lax.fori_loop(0, qi, lambda t, r: step(qi - 1 - t, r, False), runs)
    for grp in range(heads // 2):
        both = jnp.concatenate([acc_ref[2 * grp, 0:SB_DH, :], acc_ref[2 * grp + 1, SB_DH:, :]], axis=0)
        o_ref[:, grp * LANES:(grp + 1) * LANES] = both.T.astype(BF16)


def _sb_prompt(sb_bias, qn, kbf, vt, *, nb, blk, heads):
    n, w = qn.shape
    t = n // nb
    nq = t // blk
    wb = heads * SB_DH
    return pl.pallas_call(
        functools.partial(_sb_kernel, blk=blk, heads=heads),
        out_shape=jax.ShapeDtypeStruct((n, w), BF16),
        grid=(nb, w // wb, nq),
        in_specs=[pl.BlockSpec(memory_space=pltpu.SMEM),
                  pl.BlockSpec((blk, wb), lambda b, p, i: (b * nq + i, p)),
                  pl.BlockSpec((t, wb), lambda b, p, i: (b, p)),
                  pl.BlockSpec((nq, wb, blk), lambda b, p, i: (b, p, 0))],
        out_specs=pl.BlockSpec((blk, wb), lambda b, p, i: (b * nq + i, p)),
        scratch_shapes=[pltpu.VMEM((heads, LANES, blk), F32)],
        compiler_params=_params("parallel", "parallel", "arbitrary"),
        name="sb_prompt",
    )(sb_bias, qn, kbf, vt)


def _sbd_kernel(pt_ref, qbd_ref, bias_ref, knew_ref, vnew_ref, *refs, pages, tq):
    k_refs = refs[:pages]
    v_refs = refs[pages:2 * pages]
    o_ref = refs[2 * pages]
    acc_ref, run_ref = refs[2 * pages + 1:]
    j = pl.program_id(1)
    rows = SB_HEADS * tq
    psz = knew_ref.shape[0]
    qbd = qbd_ref[...]
    bias = bias_ref[...]
    r_i = lax.broadcasted_iota(jnp.int32, (psz, psz), 0)
    c_i = lax.broadcasted_iota(jnp.int32, (psz, psz), 1)
    suffix = (r_i >= c_i).astype(BF16)

    def block(kp, vp, visible):
        z = _nt(qbd, kp) + bias
        lm = -(jnp.maximum(z, 0.0) + jnp.log(1.0 + jnp.exp2(-jnp.abs(z))) * LOG2E)
        if visible is not None:
            lm = jnp.where(visible, lm, 0.0)
        hi, lo = _split(lm)
        p = _nn(hi, suffix) + _nn(lo, suffix)
        run = run_ref[...]
        a = jnp.exp2(z + run + p)
        if visible is not None:
            a = jnp.where(visible, a, 0.0)
        acc_ref[...] += _nn(a.astype(BF16), vp)
        run_ref[...] = run + jnp.broadcast_to(p[:, 0:1], run.shape)

    @pl.when(j == 0)
    def _():
        acc_ref[...] = jnp.zeros_like(acc_ref)
        run_ref[...] = jnp.zeros_like(run_ref)
        q_pos = lax.broadcasted_iota(jnp.int32, (rows, psz), 0) % tq
        k_pos = lax.broadcasted_iota(jnp.int32, (rows, psz), 1)
        block(knew_ref[...], vnew_ref[...], k_pos < q_pos)

    for p in range(pages):
        block(k_refs[p][...].astype(BF16), v_refs[p][...].astype(BF16), None)

    @pl.when(j == pl.num_programs(1) - 1)
    def _():
        acc = acc_ref[...]
        lane = lax.broadcasted_iota(jnp.int32, (tq, acc.shape[1]), 1)
        out = jnp.zeros((tq, acc.shape[1]), F32)
        for h in range(SB_HEADS):
            keep = (lane >= h * SB_DH) & (lane < (h + 1) * SB_DH)
            out = out + jnp.where(keep, acc[h * tq:(h + 1) * tq, :], 0.0)
        o_ref[...] = out.astype(BF16)


def _sb_decode(page_table, qbd, bias_rep, knew, vnew, cache_k, cache_v, *, pages, tq):
    nb, npg = page_table.shape
    psz, w = cache_k.shape[1:]
    rows = SB_HEADS * tq
    steps = npg // pages

    def page_map(p):
        return lambda b, j, pt: (pt[b, npg - 1 - (j * pages + p)], 0, 0)

    seq3 = lambda b, j, pt: (b, 0, 0)
    grid_spec = pltpu.PrefetchScalarGridSpec(
        num_scalar_prefetch=1,
        grid=(nb, steps),
        in_specs=[pl.BlockSpec((None, rows, w), seq3),
                  pl.BlockSpec((rows, psz), lambda b, j, pt: (0, 0)),
                  pl.BlockSpec((None, psz, w), seq3),
                  pl.BlockSpec((None, psz, w), seq3)]
                 + [pl.BlockSpec((None, psz, w), page_map(p)) for p in range(pages)]
                 + [pl.BlockSpec((None, psz, w), page_map(p)) for p in range(pages)],
        out_specs=pl.BlockSpec((None, tq, w), seq3),
        scratch_shapes=[pltpu.VMEM((rows, w), F32), pltpu.VMEM((rows, psz), F32)],
    )
    return pl.pallas_call(
        functools.partial(_sbd_kernel, pages=pages, tq=tq),
        out_shape=jax.ShapeDtypeStruct((nb, tq, w), BF16),
        grid_spec=grid_spec,
        compiler_params=_params("parallel", "arbitrary"),
        name="sb_decode",
    )(page_table, qbd, bias_rep, knew, vnew, *([cache_k] * pages), *([cache_v] * pages))


def _out_kernel(x_ref, hm_ref, hs_ref, ga_ref, scf_ref, shf_ref, gf_ref, wo_ref, wq_ref, ka_ref, kb_ref,
                x1_ref, hf_ref, sat_ref, sbt_ref):
    w_m = hm_ref.shape[1]
    mix = _nn(hm_ref[...], wo_ref[0:w_m, :]) + _nn(hs_ref[...], wo_ref[w_m:, :])
    x1 = x_ref[...] + ga_ref[...] * mix
    x1_ref[...] = x1
    hf = x1 * lax.rsqrt(jnp.mean(x1 * x1, axis=-1, keepdims=True) + EPS) * gf_ref[...]
    hf = (hf * (1.0 + scf_ref[...]) + shf_ref[...]).astype(BF16)
    hf_ref[...] = hf
    dk = ka_ref.shape[1]
    for h in range(PEER_HEADS):
        qa = _nn(hf, wq_ref[:, 2 * h * dk:(2 * h + 1) * dk]).astype(BF16)
        qb = _nn(hf, wq_ref[:, (2 * h + 1) * dk:(2 * h + 2) * dk]).astype(BF16)
        sat_ref[h] = _nt(ka_ref[...], qa)
        sbt_ref[h] = _nt(kb_ref[...], qb)


def _out_proj(x2d, hm, hs, ga3, scf3, shf3, g_ffn, w_out, w_query, ka, kb, *, tm, tiles_per_mod):
    n, d = x2d.shape
    nt = n // tm
    r = ga3.shape[1]
    w = hm.shape[1]
    const = lambda t: (0, 0)
    mod_map = lambda t: (t // tiles_per_mod, 0, 0)
    tile = lambda t: (t, 0)
    sc_shape = jax.ShapeDtypeStruct((PEER_HEADS, PEER_NKEYS, n), F32)
    sc_spec = pl.BlockSpec((PEER_HEADS, PEER_NKEYS, tm), lambda t: (0, 0, t))
    return pl.pallas_call(
        _out_kernel,
        out_shape=(jax.ShapeDtypeStruct((n, d), F32), jax.ShapeDtypeStruct((n, d), BF16),
                   sc_shape, sc_shape),
        grid=(nt,),
        in_specs=[pl.BlockSpec((tm, d), tile), pl.BlockSpec((tm, w), tile), pl.BlockSpec((tm, w), tile),
                  pl.BlockSpec((None, r, d), mod_map), pl.BlockSpec((None, r, d), mod_map),
                  pl.BlockSpec((None, r, d), mod_map),
                  pl.BlockSpec((1, d), const), pl.BlockSpec(w_out.shape, const),
                  pl.BlockSpec(w_query.shape, const), pl.BlockSpec(ka.shape, const),
                  pl.BlockSpec(kb.shape, const)],
        out_specs=(pl.BlockSpec((tm, d), tile), pl.BlockSpec((tm, d), tile), sc_spec, sc_spec),
        compiler_params=_params("parallel"),
        name="out_proj",
    )(x2d, hm, hs, ga3, scf3, shf3, g_ffn, w_out, w_query, ka, kb)


_CAND_ROWS = 80
_TAKEN = 2.0 ** 100


def _cand_tables():
    pos = np.full((_CAND_ROWS, LANES), 1e9, np.float32)
    for j in range(PEER_TOPK):
        pos[j] = j
    for i in range(1, 8):
        for j in range(PEER_TOPK // (i + 1)):
            pos[16 + 8 * (i - 1) + j] = i * PEER_TOPK + j
    for r in range(8):
        pos[72 + r] = (8 + r) * PEER_TOPK
    return pos


def _sel_kernel(pos_ref, sa_ref, sb_ref, cnt_ref, rb_ref, ea_ref, eb_ref, va_ref, vb_ref, *, sub):
    nk = PEER_NKEYS
    iota = lax.broadcasted_iota(jnp.int32, (nk, LANES), 0).astype(F32)
    pos = pos_ref[...]
    valid = pos < 1e8

    def pick(rem, r, v_ref):
        m = jnp.max(rem, axis=0, keepdims=True)
        idx = jnp.min(jnp.where(rem == m, iota, float(nk)), axis=0, keepdims=True)
        v_ref[pl.ds(r, 1), :] = m
        return jnp.where(iota == idx, -_TAKEN * (1.0 + r.astype(F32) * (1.0 / 64)), rem)

    def rank_of(rem):
        return jnp.where(rem <= -_TAKEN, (rem * (-1.0 / _TAKEN) - 1.0) * 64.0, float(PEER_TOPK))

    for s in range(sub):
        ls = slice(s * LANES, (s + 1) * LANES)
        sa = sa_ref[:, ls]
        sb = sb_ref[:, ls]
        rem_a, rem_b = lax.fori_loop(
            0, PEER_TOPK, lambda r, c: (pick(c[0], r, va_ref), pick(c[1], r, vb_ref)), (sa, sb))
        rank_a = rank_of(rem_a)
        rank_b = rank_of(rem_b)
        va = va_ref[...]
        vb = vb_ref[...]
        blocks = [va[0:1, :] + vb]
        for i in range(1, 8):
            blocks.append(va[i:i + 1, :] + vb[0:8, :])
        blocks.append(va[8:16, :] + vb[0:1, :])
        cand = jnp.where(valid, jnp.concatenate(blocks, axis=0), NEG_INF)
        smax = va[0:1, :] + vb[0:1, :]

        def body2(r, carry):
            cand, taken, z = carry
            m = jnp.max(cand, axis=0, keepdims=True)
            p = jnp.min(jnp.where(cand == m, pos, 1e9), axis=0, keepdims=True)
            sel = pos == p
            return (jnp.where(sel, NEG_INF, cand), jnp.where(sel, 1.0, taken), z + jnp.exp(m - smax))

        _, taken, z = lax.fori_loop(
            0, PEER_TOPK, body2, (cand, jnp.zeros_like(cand), jnp.zeros_like(smax)))

        cnt = jnp.zeros_like(sa)
        for i in range(PEER_TOPK):
            if i == 0:
                c = jnp.sum(taken[0:16, :], axis=0, keepdims=True)
            elif i < 8:
                c = jnp.sum(taken[16 + 8 * (i - 1):16 + 8 * i, :], axis=0, keepdims=True)
            else:
                c = taken[72 + (i - 8):73 + (i - 8), :]
            cnt = jnp.where(rank_a == float(i), c, cnt)
        cnt_ref[:, ls] = cnt
        rb_ref[:, ls] = rank_b.astype(BF16)
        ea_ref[:, ls] = jnp.exp(sa - va[0:1, :]) * (2.0 ** -0.5 / z)
        eb_ref[:, ls] = jnp.exp(sb - vb[0:1, :]).astype(BF16)


def _peer_select(pos, sat, sbt, *, ts):
    nh, nk, n = sat.shape
    spec = pl.BlockSpec((None, nk, ts), lambda t, h: (h, 0, t))
    shp = jax.ShapeDtypeStruct((nh, nk, n), F32)
    shp_bf = jax.ShapeDtypeStruct((nh, nk, n), BF16)
    return pl.pallas_call(
        functools.partial(_sel_kernel, sub=ts // LANES),
        out_shape=(shp, shp_bf, shp, shp_bf),
        grid=(n // ts, nh),
        in_specs=[pl.BlockSpec(pos.shape, lambda t, h: (0, 0)), spec, spec],
        out_specs=(spec, spec, spec, spec),
        scratch_shapes=[pltpu.VMEM((PEER_TOPK, LANES), F32), pltpu.VMEM((PEER_TOPK, LANES), F32)],
        compiler_params=_params("parallel", "parallel"),
        name="peer_select",
    )(pos, sat, sbt)


def _peer_kernel(hf_ref, u_ref, vt_ref, cnt_ref, rb_ref, ea_ref, eb_ref, x1_ref, gf_ref, y_ref, acc_ref,
                 *, na, tc):
    e = pl.program_id(1)
    nk = PEER_NKEYS
    tm = hf_ref.shape[0]
    chunks = [slice(c * tc, (c + 1) * tc) for c in range(tm // tc)]

    @pl.when(e == 0)
    def _():
        acc_ref[...] = jnp.zeros_like(acc_ref)

    def row(ref, h, a, cs):
        r = jnp.broadcast_to(ref[h, pl.ds(a, 1), cs], (_BF16_ROWS, tc)).astype(BF16)
        return jnp.tile(r, (nk // _BF16_ROWS, 1))

    acts = [_nt(u_ref[...], hf_ref[cs, :]) for cs in chunks]
    zero = jnp.zeros((nk, tc), BF16)
    for cs, act in zip(chunks, acts):
        gel = (act * (1.0 + lax.erf(act))).astype(BF16)
        slabs = []
        for ai in range(na):
            a = e * na + ai
            g = zero
            for h in range(PEER_HEADS):
                taken = rb_ref[h, :, cs] < row(cnt_ref, h, a, cs)
                g = g + jnp.where(taken, eb_ref[h, :, cs], zero) * row(ea_ref, h, a, cs)
            slabs.append(g * gel[ai * nk:(ai + 1) * nk, :])
        acc_ref[:, cs] += _nn(vt_ref[...], jnp.concatenate(slabs, axis=0))

    @pl.when(e == pl.num_programs(1) - 1)
    def _():
        y_ref[...] = x1_ref[...] + gf_ref[...] * acc_ref[...].T


def _peer_experts(hf, u_bf, vt_bf, cnt, rb, ea, eb, x1, gf3, *, tm, eb_sz, tiles_per_mod):
    n, d = hf.shape
    nblk = u_bf.shape[0] // eb_sz
    r = gf3.shape[1]
    na = eb_sz // PEER_NKEYS
    sel = pl.BlockSpec((PEER_HEADS, PEER_NKEYS, tm), lambda t, e: (0, 0, t))
    return pl.pallas_call(
        functools.partial(_peer_kernel, na=na, tc=min(PEER_TC, tm)),
        out_shape=jax.ShapeDtypeStruct((n, d), F32),
        grid=(n // tm, nblk),
        in_specs=[pl.BlockSpec((tm, d), lambda t, e: (t, 0)),
                  pl.BlockSpec((eb_sz, d), lambda t, e: (e, 0)),
                  pl.BlockSpec((d, eb_sz), lambda t, e: (0, e)),
                  sel, sel, sel, sel,
                  pl.BlockSpec((tm, d), lambda t, e: (t, 0)),
                  pl.BlockSpec((None, r, d), lambda t, e: (t // tiles_per_mod, 0, 0))],
        out_specs=pl.BlockSpec((tm, d), lambda t, e: (t, 0)),
        scratch_shapes=[pltpu.VMEM((d, tm), F32)],
        compiler_params=_params("parallel", "arbitrary"),
        name="peer_experts",
    )(hf, u_bf, vt_bf, cnt, rb, ea, eb, x1, gf3)


IN_TM = 256
ML_CHUNK = 256
ML_CHUNK_DEC = 128
SB_DEC_PAGES = 4
SB_PROMPT_HEADS = 4
SEL_TS = 512
PEER_TM = 1024
PEER_EB = 512
PEER_TC = 512


def _layer_weights(w_in, b_gates, qnorm_g, knorm_g, w_out, w_query, sub_keys_a, sub_keys_b, expert_u,
                   expert_v):
    w_m = ML_HEADS * ML_DH
    ng = 2 * ML_HEADS
    g0 = 4 * w_m
    d = w_in.shape[0]
    w_main = jnp.concatenate([w_in[:, :g0], w_in[:, g0 + ng:]], axis=1).astype(BF16)
    wg = w_in[:, g0:g0 + ng]
    w_gate = jnp.pad(wg, ((0, 0), (0, LANES - ng))).astype(BF16)
    w_gate_t = wg.T.astype(BF16)
    bg_row = jnp.pad(b_gates, (0, LANES - ng)).reshape(1, LANES)
    bg_col = b_gates.reshape(ng, 1)
    head = np.arange(SB_HEADS * SB_DH) // SB_DH
    bd = jnp.asarray(head[:, None] == head[None, :], BF16)
    return dict(w_main=w_main, w_gate=w_gate, w_gate_t=w_gate_t, bg_row=bg_row, bg_col=bg_col,
                gq=qnorm_g.reshape(1, -1), gk=knorm_g.reshape(1, -1), bd=bd,
                w_out=w_out.astype(BF16), w_query=w_query.astype(BF16),
                ka=sub_keys_a.astype(BF16), kb=sub_keys_b.astype(BF16),
                u=(expert_u * 2.0 ** -0.5).astype(BF16), vt=expert_v.T.astype(BF16), d=d)


def kernel(x_prompt, x_sample, cache_sb_k, cache_sb_v, state_mlstm_C, state_mlstm_n, state_mlstm_m,
           page_table, c_prompt, c_sample, w_ada, b_ada, norm_mix_g, norm_ffn_g, w_in, b_gates, qnorm_g,
           knorm_g, sb_bias, mlstm_head_g, w_out, w_query, sub_keys_a, sub_keys_b, expert_u, expert_v):
    depth = w_ada.shape[0]
    assert depth == 1, "single-layer trunk"
    bp, tp, d = x_prompt.shape
    bs, ts, _ = x_sample.shape
    w_m = ML_HEADS * ML_DH
    w_s = SB_HEADS * SB_DH
    psz = cache_sb_k.shape[2]

    lw = _layer_weights(w_in[0], b_gates[0], qnorm_g[0], knorm_g[0], w_out[0], w_query[0], sub_keys_a[0],
                        sub_keys_b[0], expert_u[0], expert_v[0])
    g_mix = norm_mix_g[0].reshape(1, d)
    g_ffn = norm_ffn_g[0].reshape(1, d)
    gh = mlstm_head_g[0].reshape(1, w_m)
    bias = sb_bias[0].astype(F32)
    pos = jnp.asarray(_cand_tables())

    nrow = bp + bs
    pad = (-nrow) % 8
    c_all = jnp.pad(jnp.concatenate([c_prompt, c_sample], axis=0).astype(F32), ((0, pad), (0, 0)))
    mod = _ada(c_all, w_ada[0].astype(BF16), b_ada[0].reshape(1, -1))
    mods = [mod[:, i * d:(i + 1) * d] for i in range(6)]

    def group_mods(lo, hi, per_token):
        out = []
        for m in mods:
            m = m[lo:hi]
            if per_token:
                m = jnp.repeat(m, ts, axis=0).reshape(1, (hi - lo) * ts, d)
            else:
                m = m.reshape(hi - lo, 1, d)
            out.append(m)
        return out

    def peer(hf, x1, gf3, sat, sbt, n_tok, tm, tpm):
        sel_ts = min(SEL_TS, n_tok)
        cnt, rb, ea, eb = _peer_select(pos, sat, sbt, ts=sel_ts)
        return _peer_experts(hf, lw["u"], lw["vt"], cnt, rb, ea, eb, x1, gf3, tm=tm, eb_sz=PEER_EB,
                             tiles_per_mod=tpm)

    sh_a, sc_a, ga, sh_f, sc_f, gf = group_mods(0, bp, False)
    xp = x_prompt.reshape(bp * tp, d).astype(F32)
    tpm = tp // IN_TM
    (mq, mk, mv, og, gcol, grow, qn, kbf, ks_p, vs_p, vt) = _in_proj(
        xp, sc_a, sh_a, g_mix, lw["w_main"], lw["w_gate"], lw["w_gate_t"], lw["bg_row"], lw["bg_col"],
        lw["gq"], lw["gk"], lw["bd"], tm=IN_TM, tiles_per_mod=tpm)
    zc = jnp.zeros((bp, ML_HEADS, ML_DH, ML_DH), F32)
    zn = jnp.zeros((bp, ML_HEADS, ML_DH), F32)
    hm, c_p, n_p, m_p = _mlstm(mq, mk, mv, og, gcol, grow, zc, zn, zn, gh, nb=bp, chunk=ML_CHUNK)
    hs = _sb_prompt(bias, qn, kbf, vt, nb=bp, blk=IN_TM, heads=SB_PROMPT_HEADS)
    x1, hf, sat, sbt = _out_proj(xp, hm, hs, ga, sc_f, sh_f, g_ffn, lw["w_out"], lw["w_query"], lw["ka"],
                                 lw["kb"], tm=IN_TM, tiles_per_mod=tpm)
    y_p = peer(hf, x1, gf, sat, sbt, bp * tp, PEER_TM, tp // PEER_TM)

    sh_a, sc_a, ga, sh_f, sc_f, gf = group_mods(bp, bp + bs, True)
    ns = bs * ts
    xs = x_sample.reshape(ns, d).astype(F32)
    (mq, mk, mv, og, gcol, grow, qn, kbf, ks_s, vs_s, _) = _in_proj(
        xs, sc_a, sh_a, g_mix, lw["w_main"], lw["w_gate"], lw["w_gate_t"], lw["bg_row"], lw["bg_col"],
        lw["gq"], lw["gk"], lw["bd"], tm=ns, tiles_per_mod=1)

    lc = ML_CHUNK_DEC

    def pad_seq(a, fill=0.0):
        a = a.reshape(bs, ts, a.shape[-1])
        return jnp.pad(a, ((0, 0), (0, lc - ts), (0, 0)), constant_values=fill).reshape(bs * lc, -1)

    gcol_fill = jnp.where(jnp.arange(LANES) < ML_HEADS, -1e30, 0.0).astype(F32)
    gcol_p = gcol.reshape(bs, ts, LANES)
    gcol_p = jnp.concatenate(
        [gcol_p, jnp.broadcast_to(gcol_fill, (bs, lc - ts, LANES))], axis=1).reshape(bs * lc, LANES)
    grow_fill = jnp.where(jnp.arange(8) < ML_HEADS, -1e30, 0.0).astype(F32)
    grow_p = grow.reshape(8, bs, ts)
    grow_p = jnp.concatenate(
        [grow_p, jnp.broadcast_to(grow_fill[:, None, None], (8, bs, lc - ts))], axis=2).reshape(8, bs * lc)
    m0 = jnp.broadcast_to(state_mlstm_m[0].astype(F32)[:, :, None], (bs, ML_HEADS, ML_DH))
    hm_pad, c_s, n_s, m_s = _mlstm(pad_seq(mq), pad_seq(mk), pad_seq(mv), pad_seq(og), gcol_p, grow_p,
                                   state_mlstm_C[0].astype(F32), state_mlstm_n[0].astype(F32), m0, gh,
                                   nb=bs, chunk=lc)
    hm = hm_pad.reshape(bs, lc, w_m)[:, :ts].reshape(ns, w_m)

    lane_head = jnp.arange(w_s) // SB_DH
    q3 = qn.reshape(bs, 1, ts, w_s)
    qbd = jnp.where(lane_head[None, None, None, :] == jnp.arange(SB_HEADS)[None, :, None, None], q3,
                    jnp.zeros_like(q3)).reshape(bs, SB_HEADS * ts, w_s)
    bias_rep = jnp.broadcast_to(jnp.repeat(bias * LOG2E, ts)[:, None], (SB_HEADS * ts, psz))
    knew = jnp.pad(kbf.reshape(bs, ts, w_s), ((0, 0), (0, psz - ts), (0, 0)))
    vnew = jnp.pad(vs_s.astype(BF16).reshape(bs, ts, w_s), ((0, 0), (0, psz - ts), (0, 0)))
    ck = cache_sb_k[0].reshape(cache_sb_k.shape[1], psz, w_s)
    cv = cache_sb_v[0].reshape(cache_sb_v.shape[1], psz, w_s)
    hs = _sb_decode(page_table.astype(jnp.int32), qbd, bias_rep, knew, vnew, ck, cv, pages=SB_DEC_PAGES,
                    tq=ts).reshape(ns, w_s)
    x1, hf, sat, sbt = _out_proj(xs, hm, hs, ga, sc_f, sh_f, g_ffn, lw["w_out"], lw["w_query"], lw["ka"],
                                 lw["kb"], tm=ns, tiles_per_mod=1)
    y_s = peer(hf, x1, gf, sat, sbt, ns, ns, 1)

    dt_k, dt_v = cache_sb_k.dtype, cache_sb_v.dtype
    return (y_p.reshape(bp, tp, d).astype(x_prompt.dtype),
            y_s.reshape(bs, ts, d).astype(x_sample.dtype),
            ks_p.reshape(1, bp, tp, SB_HEADS, SB_DH).astype(dt_k),
            vs_p.reshape(1, bp, tp, SB_HEADS, SB_DH).astype(dt_v),
            c_p[None], n_p[None], m_p[None, :, :, 0],
            ks_s.reshape(1, bs, ts, SB_HEADS, SB_DH).astype(dt_k),
            vs_s.reshape(1, bs, ts, SB_HEADS, SB_DH).astype(dt_v),
            c_s[None], n_s[None], m_s[None, :, :, 0])
```

```python
import functools

import numpy as np
import jax
import jax.numpy as jnp
from jax import lax
from jax.experimental import pallas as pl
from jax.experimental.pallas import tpu as pltpu

F32 = jnp.float32
BF16 = jnp.bfloat16

EPS = 1e-6
GATE_SOFTCAP = 15.0
SB_HEADS = 8
SB_DH = 64
ML_HEADS = 4
ML_DH = 128
PEER_HEADS = 8
PEER_NKEYS = 128
PEER_TOPK = 16

LOG2E = 1.4426950408889634
LANES = 128
_BF16_ROWS = 16
VMEM_LIMIT = 56 * 1024 * 1024

NEG_INF = float("-inf")


def _nn(a, b):
    return jnp.dot(a, b, preferred_element_type=F32)


def _nt(a, b):
    return lax.dot_general(a, b, (((1,), (1,)), ((), ())), preferred_element_type=F32)


def _split(x):
    hi = x.astype(BF16)
    lo = (x - hi.astype(F32)).astype(BF16)
    return hi, lo


def _softplus(z):
    return jnp.maximum(z, 0.0) + jnp.log1p(jnp.exp(-jnp.abs(z)))


def _log_sigmoid(z):
    return -_softplus(-z)


def _softcap(a):
    return GATE_SOFTCAP * jnp.tanh(a / GATE_SOFTCAP)


def _params(*sem):
    return pltpu.CompilerParams(dimension_semantics=sem, vmem_limit_bytes=VMEM_LIMIT)


def _ada_kernel(c_ref, w_ref, b_ref, o_ref):
    c = c_ref[...]
    a = (c * jax.nn.sigmoid(c)).astype(BF16)
    o_ref[...] = _nn(a, w_ref[...]) + b_ref[...]


def _ada(c_all, w_ada, b_ada):
    rows, d = c_all.shape
    n_out = w_ada.shape[1]
    return pl.pallas_call(
        _ada_kernel,
        out_shape=jax.ShapeDtypeStruct((rows, n_out), F32),
        grid=(n_out // d,),
        in_specs=[pl.BlockSpec((rows, d), lambda j: (0, 0)),
                  pl.BlockSpec((d, d), lambda j: (0, j)),
                  pl.BlockSpec((1, d), lambda j: (0, j))],
        out_specs=pl.BlockSpec((rows, d), lambda j: (0, j)),
        compiler_params=_params("parallel"),
        name="ada_mod",
    )(c_all, w_ada, b_ada)


def _in_kernel(x_ref, sc_ref, sh_ref, g_ref, w_ref, wg_ref, wgt_ref, bgr_ref, bgc_ref, gq_ref, gk_ref,
               bd_ref, mq_ref, mk_ref, mv_ref, og_ref, gcol_ref, grow_ref, qn_ref, kbf_ref, ks_ref,
               vs_ref, vt_ref):
    w_m = ML_HEADS * ML_DH
    x = x_ref[...]
    h = x * lax.rsqrt(jnp.mean(x * x, axis=-1, keepdims=True) + EPS) * g_ref[...]
    h = h * (1.0 + sc_ref[...]) + sh_ref[...]
    hb = h.astype(BF16)

    def seg(k):
        return _nn(hb, w_ref[:, k * w_m:(k + 1) * w_m])

    mq_ref[...] = seg(0).astype(BF16)
    mk_ref[...] = (seg(1) * ML_DH ** -0.5).astype(BF16)
    mv_ref[...] = seg(2).astype(BF16)
    og_ref[...] = jax.nn.sigmoid(seg(3)).astype(BF16)

    gc = _softcap(_nn(hb, wg_ref[...]) + bgr_ref[...])
    lane = lax.broadcasted_iota(jnp.int32, gc.shape, 1)
    gcol_ref[...] = jnp.where(lane < ML_HEADS, gc, _log_sigmoid(gc))
    gr = _softcap(_nt(wgt_ref[...], hb) + bgc_ref[...])
    row = lax.broadcasted_iota(jnp.int32, gr.shape, 0)
    grow_ref[...] = jnp.where(row < ML_HEADS, gr, _log_sigmoid(gr))

    def head_norm(s, g):
        hi, lo = _split(s * s)
        ss = _nn(hi, bd_ref[...]) + _nn(lo, bd_ref[...])
        return s * lax.rsqrt(ss * (1.0 / SB_DH) + EPS) * g

    qn_ref[...] = (head_norm(seg(4), gq_ref[...]) * (SB_DH ** -0.5 * LOG2E)).astype(BF16)
    kn = head_norm(seg(5), gk_ref[...])
    ks_ref[...] = kn
    kbf_ref[...] = kn.astype(BF16)
    sv = seg(6)
    vs_ref[...] = sv
    for p in range(w_m // LANES):
        vt_ref[p * LANES:(p + 1) * LANES, :] = sv[:, p * LANES:(p + 1) * LANES].T.astype(BF16)


def _in_proj(x2d, sc3, sh3, g_mix, w_main, w_gate, w_gate_t, bg_row, bg_col, gq, gk, bd, *, tm,
             tiles_per_mod):
    n, d = x2d.shape
    nt = n // tm
    r = sc3.shape[1]
    w = ML_HEADS * ML_DH
    const = lambda t: (0, 0)
    mod_map = lambda t: (t // tiles_per_mod, 0, 0)
    tile = lambda t: (t, 0)
    tok = lambda dt: jax.ShapeDtypeStruct((n, w), dt)
    return pl.pallas_call(
        _in_kernel,
        out_shape=(tok(BF16), tok(BF16), tok(BF16), tok(BF16),
                   jax.ShapeDtypeStruct((n, LANES), F32), jax.ShapeDtypeStruct((8, n), F32),
                   tok(BF16), tok(BF16), tok(F32), tok(F32),
                   jax.ShapeDtypeStruct((nt, w, tm), BF16)),
        grid=(nt,),
        in_specs=[pl.BlockSpec((tm, d), tile),
                  pl.BlockSpec((None, r, d), mod_map),
                  pl.BlockSpec((None, r, d), mod_map),
                  pl.BlockSpec((1, d), const),
                  pl.BlockSpec(w_main.shape, const),
                  pl.BlockSpec(w_gate.shape, const),
                  pl.BlockSpec(w_gate_t.shape, const),
                  pl.BlockSpec(bg_row.shape, const),
                  pl.BlockSpec(bg_col.shape, const),
                  pl.BlockSpec((1, w), const),
                  pl.BlockSpec((1, w), const),
                  pl.BlockSpec((w, w), const)],
        out_specs=(pl.BlockSpec((tm, w), tile), pl.BlockSpec((tm, w), tile), pl.BlockSpec((tm, w), tile),
                   pl.BlockSpec((tm, w), tile), pl.BlockSpec((tm, LANES), tile),
                   pl.BlockSpec((8, tm), lambda t: (0, t)),
                   pl.BlockSpec((tm, w), tile), pl.BlockSpec((tm, w), tile), pl.BlockSpec((tm, w), tile),
                   pl.BlockSpec((tm, w), tile), pl.BlockSpec((None, w, tm), lambda t: (t, 0, 0))),
        compiler_params=_params("parallel"),
        name="in_proj",
    )(x2d, sc3, sh3, g_mix, w_main, w_gate, w_gate_t, bg_row, bg_col, gq, gk, bd)


def _mlstm_kernel(q_ref, k_ref, v_ref, og_ref, gcol_ref, grow_ref, c0_ref, n0_ref, m0_ref, gh_ref,
                  hm_ref, c_ref, n_ref, m_ref, *, chunk):
    L = chunk

    @pl.when(pl.program_id(1) == 0)
    def _():
        c_ref[...] = c0_ref[...]
        n_ref[...] = n0_ref[...]
        m_ref[...] = m0_ref[...]

    t_i = lax.broadcasted_iota(jnp.int32, (L, L), 0)
    s_i = lax.broadcasted_iota(jnp.int32, (L, L), 1)
    causal = s_i <= t_i
    tri_lo = causal.astype(BF16)
    tri_up = (t_i <= s_i).astype(BF16)

    gcol = gcol_ref[...]
    grow = grow_ref[...]
    chi, clo = _split(gcol)
    bc = _nn(tri_lo, chi) + _nn(tri_lo, clo)
    rhi, rlo = _split(grow)
    br = _nn(rhi, tri_up) + _nn(rlo, tri_up)

    for h in range(ML_HEADS):
        sl = slice(h * ML_DH, (h + 1) * ML_DH)
        icol = gcol[:, h:h + 1]
        bcol = bc[:, ML_HEADS + h:ML_HEADS + h + 1]
        irow = grow[h:h + 1, :]
        brow = br[ML_HEADS + h:ML_HEADS + h + 1, :]
        m_prev = m_ref[h:h + 1, 0:1]

        dmat = jnp.where(causal, bcol - brow + irow, NEG_INF)
        inter = bcol + m_prev
        mt = jnp.maximum(inter, jnp.max(dmat, axis=1, keepdims=True))
        w_intra = jnp.exp(dmat - mt)
        w_inter = jnp.exp(inter - mt)

        q = q_ref[:, sl]
        k = k_ref[:, sl]
        v = v_ref[:, sl]
        sw = _nt(q, k) * w_intra
        n_row = n_ref[h:h + 1, :]
        num = w_inter * _nt(q, c_ref[h].astype(BF16)) + _nn(sw.astype(BF16), v)
        qn = jnp.sum(q.astype(F32) * n_row, axis=1, keepdims=True)
        dot = w_inter * qn + jnp.sum(sw, axis=1, keepdims=True)
        hh = num / jnp.maximum(jnp.abs(dot), jnp.exp(-mt))
        hn = hh * lax.rsqrt(jnp.mean(hh * hh, axis=1, keepdims=True) + EPS) * gh_ref[:, sl]
        hm_ref[:, sl] = (hn * og_ref[:, sl].astype(F32)).astype(BF16)

        m_last = mt[L - 1:L, :]
        b_last = bcol[L - 1:L, :]
        decay = jnp.exp(b_last - bcol + icol - m_last)
        carry = jnp.exp(b_last + m_prev - m_last)
        vd = (v.astype(F32) * decay).T.astype(BF16)
        c_ref[h] = carry * c_ref[h] + _nn(vd, k)
        n_ref[h:h + 1, :] = carry * n_row + jnp.sum(decay * k.astype(F32), axis=0, keepdims=True)
        m_ref[h:h + 1, :] = jnp.broadcast_to(m_last, (1, ML_DH))


def _mlstm(mq, mk, mv, og, gcol, grow, c0, n0, m0, gh, *, nb, chunk):
    n, w = mq.shape
    nc = n // (nb * chunk)
    blk = lambda b, c: (b * nc + c, 0)
    st4 = lambda b, c: (b, 0, 0, 0)
    st3 = lambda b, c: (b, 0, 0)
    return pl.pallas_call(
        functools.partial(_mlstm_kernel, chunk=chunk),
        out_shape=(jax.ShapeDtypeStruct((n, w), BF16),
                   jax.ShapeDtypeStruct((nb, ML_HEADS, ML_DH, ML_DH), F32),
                   jax.ShapeDtypeStruct((nb, ML_HEADS, ML_DH), F32),
                   jax.ShapeDtypeStruct((nb, ML_HEADS, ML_DH), F32)),
        grid=(nb, nc),
        in_specs=[pl.BlockSpec((chunk, w), blk), pl.BlockSpec((chunk, w), blk),
                  pl.BlockSpec((chunk, w), blk), pl.BlockSpec((chunk, w), blk),
                  pl.BlockSpec((chunk, LANES), blk),
                  pl.BlockSpec((8, chunk), lambda b, c: (0, b * nc + c)),
                  pl.BlockSpec((None, ML_HEADS, ML_DH, ML_DH), st4),
                  pl.BlockSpec((None, ML_HEADS, ML_DH), st3),
                  pl.BlockSpec((None, ML_HEADS, ML_DH), st3),
                  pl.BlockSpec((1, w), lambda b, c: (0, 0))],
        out_specs=(pl.BlockSpec((chunk, w), blk),
                   pl.BlockSpec((None, ML_HEADS, ML_DH, ML_DH), st4),
                   pl.BlockSpec((None, ML_HEADS, ML_DH), st3),
                   pl.BlockSpec((None, ML_HEADS, ML_DH), st3)),
        compiler_params=_params("parallel", "arbitrary"),
        name="mlstm",
    )(mq, mk, mv, og, gcol, grow, c0, n0, m0, gh)


def _sb_kernel(bias_ref, q_ref, k_ref, vt_ref, o_ref, acc_ref, *, blk, heads):
    grp0 = pl.program_id(1)
    qi = pl.program_id(2)
    lane = lax.broadcasted_iota(jnp.int32, (blk, LANES), 1)
    r_i = lax.broadcasted_iota(jnp.int32, (blk, blk), 0)
    c_i = lax.broadcasted_iota(jnp.int32, (blk, blk), 1)
    suffix = (c_i >= r_i).astype(BF16)
    visible = r_i < c_i

    q_neg, bias_neg = [], []
    for hh in range(heads):
        grp, half = divmod(hh, 2)
        q2 = q_ref[:, grp * LANES:(grp + 1) * LANES]
        keep = (lane >= half * SB_DH) & (lane < (half + 1) * SB_DH)
        q_neg.append(jnp.where(keep, -q2, jnp.zeros_like(q2)))
        bias_neg.append(-LOG2E * bias_ref[heads * grp0 + hh])
    acc_ref[...] = jnp.zeros_like(acc_ref)

    def step(js, runs, masked):
        hs = range(heads)
        xs = range(len(js))
        starts = [pl.multiple_of(j * blk, blk) for j in js]
        gl = [slice((hh // 2) * LANES, (hh // 2 + 1) * LANES) for hh in hs]
        zn = [[_nt(k_ref[pl.ds(starts[x], blk), gl[hh]], q_neg[hh]) + bias_neg[hh] for hh in hs]
              for x in xs]
        lm = [[jnp.minimum(z, 0.0) - jnp.log(1.0 + jnp.exp2(-jnp.abs(z))) * LOG2E for z in row]
              for row in zn]
        if masked:
            lm = [[jnp.where(visible, v, 0.0) for v in row] for row in lm]
        parts = [[_split(v) for v in row] for row in lm]
        p = [[_nn(suffix, hi) + _nn(suffix, lo) for hi, lo in row] for row in parts]
        for x in xs:
            a = [jnp.exp2((runs[hh] + p[x][hh]) - zn[x][hh]) for hh in hs]
            if masked:
                a = [jnp.where(visible, v, 0.0) for v in a]
            for hh in hs:
                acc_ref[hh] += _nn(vt_ref[js[x], gl[hh], :], a[hh].astype(BF16))
            runs = tuple(runs[hh] + p[x][hh][0:1, :] for hh in hs)
        return runs

    runs = step([qi], tuple(jnp.zeros((1, blk), F32) for _ in range(heads)), True)
    odd = qi % 2
    runs = lax.cond(odd == 1, lambda r: step([qi - 1], r, False), lambda r: r, runs)
    base = qi - 1 - odd
    lax.fori_loop(0, (qi - odd) // 2, lambda t, r: step([base - 2 * t, base - 2 * t - 1], r, False), runs)
    for grp in range(heads // 2):
        both = jnp.concatenate([acc_ref[2 * grp, 0:SB_DH, :], acc_ref[2 * grp + 1, SB_DH:, :]], axis=0)
        o_ref[:, grp * LANES:(grp + 1) * LANES] = both.T.astype(BF16)


def _sb_prompt(sb_bias, qn, kbf, vt, *, nb, blk, heads):
    n, w = qn.shape
    t = n // nb
    nq = t // blk
    wb = heads * SB_DH
    return pl.pallas_call(
        functools.partial(_sb_kernel, blk=blk, heads=heads),
        out_shape=jax.ShapeDtypeStruct((n, w), BF16),
        grid=(nb, w // wb, nq),
        in_specs=[pl.BlockSpec(memory_space=pltpu.SMEM),
                  pl.BlockSpec((blk, wb), lambda b, p, i: (b * nq + i, p)),
                  pl.BlockSpec((t, wb), lambda b, p, i: (b, p)),
                  pl.BlockSpec((nq, wb, blk), lambda b, p, i: (b, p, 0))],
        out_specs=pl.BlockSpec((blk, wb), lambda b, p, i: (b * nq + i, p)),
        scratch_shapes=[pltpu.VMEM((heads, LANES, blk), F32)],
        compiler_params=_params("parallel", "parallel", "arbitrary"),
        name="sb_prompt",
    )(sb_bias, qn, kbf, vt)


def _sbd_kernel(pt_ref, qbd_ref, bias_ref, knew_ref, vnew_ref, *refs, pages, tq):
    k_refs = refs[:pages]
    v_refs = refs[pages:2 * pages]
    o_ref = refs[2 * pages]
    acc_ref, run_ref = refs[2 * pages + 1:]
    j = pl.program_id(1)
    rows = SB_HEADS * tq
    psz = knew_ref.shape[0]
    qbd = qbd_ref[...]
    bias = bias_ref[...]
    r_i = lax.broadcasted_iota(jnp.int32, (psz, psz), 0)
    c_i = lax.broadcasted_iota(jnp.int32, (psz, psz), 1)
    suffix = (r_i >= c_i).astype(BF16)

    def block(kp, vp, visible):
        z = _nt(qbd, kp) + bias
        lm = -(jnp.maximum(z, 0.0) + jnp.log(1.0 + jnp.exp2(-jnp.abs(z))) * LOG2E)
        if visible is not None:
            lm = jnp.where(visible, lm, 0.0)
        hi, lo = _split(lm)
        p = _nn(hi, suffix) + _nn(lo, suffix)
        run = run_ref[...]
        a = jnp.exp2(z + run + p)
        if visible is not None:
            a = jnp.where(visible, a, 0.0)
        acc_ref[...] += _nn(a.astype(BF16), vp)
        run_ref[...] = run + jnp.broadcast_to(p[:, 0:1], run.shape)

    @pl.when(j == 0)
    def _():
        acc_ref[...] = jnp.zeros_like(acc_ref)
        run_ref[...] = jnp.zeros_like(run_ref)
        q_pos = lax.broadcasted_iota(jnp.int32, (rows, psz), 0) % tq
        k_pos = lax.broadcasted_iota(jnp.int32, (rows, psz), 1)
        block(knew_ref[...], vnew_ref[...], k_pos < q_pos)

    for p in range(pages):
        block(k_refs[p][...].astype(BF16), v_refs[p][...].astype(BF16), None)

    @pl.when(j == pl.num_programs(1) - 1)
    def _():
        acc = acc_ref[...]
        lane = lax.broadcasted_iota(jnp.int32, (tq, acc.shape[1]), 1)
        out = jnp.zeros((tq, acc.shape[1]), F32)
        for h in range(SB_HEADS):
            keep = (lane >= h * SB_DH) & (lane < (h + 1) * SB_DH)
            out = out + jnp.where(keep, acc[h * tq:(h + 1) * tq, :], 0.0)
        o_ref[...] = out.astype(BF16)


def _sb_decode(page_table, qbd, bias_rep, knew, vnew, cache_k, cache_v, *, pages, tq):
    nb, npg = page_table.shape
    psz, w = cache_k.shape[1:]
    rows = SB_HEADS * tq
    steps = npg // pages

    def page_map(p):
        return lambda b, j, pt: (pt[b, npg - 1 - (j * pages + p)], 0, 0)

    seq3 = lambda b, j, pt: (b, 0, 0)
    grid_spec = pltpu.PrefetchScalarGridSpec(
        num_scalar_prefetch=1,
        grid=(nb, steps),
        in_specs=[pl.BlockSpec((None, rows, w), seq3),
                  pl.BlockSpec((rows, psz), lambda b, j, pt: (0, 0)),
                  pl.BlockSpec((None, psz, w), seq3),
                  pl.BlockSpec((None, psz, w), seq3)]
                 + [pl.BlockSpec((None, psz, w), page_map(p)) for p in range(pages)]
                 + [pl.BlockSpec((None, psz, w), page_map(p)) for p in range(pages)],
        out_specs=pl.BlockSpec((None, tq, w), seq3),
        scratch_shapes=[pltpu.VMEM((rows, w), F32), pltpu.VMEM((rows, psz), F32)],
    )
    return pl.pallas_call(
        functools.partial(_sbd_kernel, pages=pages, tq=tq),
        out_shape=jax.ShapeDtypeStruct((nb, tq, w), BF16),
        grid_spec=grid_spec,
        compiler_params=_params("parallel", "arbitrary"),
        name="sb_decode",
    )(page_table, qbd, bias_rep, knew, vnew, *([cache_k] * pages), *([cache_v] * pages))


def _out_kernel(x_ref, hm_ref, hs_ref, ga_ref, scf_ref, shf_ref, gf_ref, wo_ref, wq_ref, ka_ref, kb_ref,
                x1_ref, hf_ref, sat_ref, sbt_ref):
    w_m = hm_ref.shape[1]
    mix = _nn(hm_ref[...], wo_ref[0:w_m, :]) + _nn(hs_ref[...], wo_ref[w_m:, :])
    x1 = x_ref[...] + ga_ref[...] * mix
    x1_ref[...] = x1
    hf = x1 * lax.rsqrt(jnp.mean(x1 * x1, axis=-1, keepdims=True) + EPS) * gf_ref[...]
    hf = (hf * (1.0 + scf_ref[...]) + shf_ref[...]).astype(BF16)
    hf_ref[...] = hf
    dk = ka_ref.shape[1]
    for h in range(PEER_HEADS):
        qa = _nn(hf, wq_ref[:, 2 * h * dk:(2 * h + 1) * dk]).astype(BF16)
        qb = _nn(hf, wq_ref[:, (2 * h + 1) * dk:(2 * h + 2) * dk]).astype(BF16)
        sat_ref[h] = _nt(ka_ref[...], qa)
        sbt_ref[h] = _nt(kb_ref[...], qb)


def _out_proj(x2d, hm, hs, ga3, scf3, shf3, g_ffn, w_out, w_query, ka, kb, *, tm, tiles_per_mod):
    n, d = x2d.shape
    nt = n // tm
    r = ga3.shape[1]
    w = hm.shape[1]
    const = lambda t: (0, 0)
    mod_map = lambda t: (t // tiles_per_mod, 0, 0)
    tile = lambda t: (t, 0)
    sc_shape = jax.ShapeDtypeStruct((PEER_HEADS, PEER_NKEYS, n), F32)
    sc_spec = pl.BlockSpec((PEER_HEADS, PEER_NKEYS, tm), lambda t: (0, 0, t))
    return pl.pallas_call(
        _out_kernel,
        out_shape=(jax.ShapeDtypeStruct((n, d), F32), jax.ShapeDtypeStruct((n, d), BF16),
                   sc_shape, sc_shape),
        grid=(nt,),
        in_specs=[pl.BlockSpec((tm, d), tile), pl.BlockSpec((tm, w), tile), pl.BlockSpec((tm, w), tile),
                  pl.BlockSpec((None, r, d), mod_map), pl.BlockSpec((None, r, d), mod_map),
                  pl.BlockSpec((None, r, d), mod_map),
                  pl.BlockSpec((1, d), const), pl.BlockSpec(w_out.shape, const),
                  pl.BlockSpec(w_query.shape, const), pl.BlockSpec(ka.shape, const),
                  pl.BlockSpec(kb.shape, const)],
        out_specs=(pl.BlockSpec((tm, d), tile), pl.BlockSpec((tm, d), tile), sc_spec, sc_spec),
        compiler_params=_params("parallel"),
        name="out_proj",
    )(x2d, hm, hs, ga3, scf3, shf3, g_ffn, w_out, w_query, ka, kb)


_CAND_ROWS = 80
_TAKEN = 2.0 ** 100


def _cand_tables():
    pos = np.full((_CAND_ROWS, LANES), 1e9, np.float32)
    for j in range(PEER_TOPK):
        pos[j] = j
    for i in range(1, 8):
        for j in range(PEER_TOPK // (i + 1)):
            pos[16 + 8 * (i - 1) + j] = i * PEER_TOPK + j
    for r in range(8):
        pos[72 + r] = (8 + r) * PEER_TOPK
    return pos


def _sel_kernel(pos_ref, sa_ref, sb_ref, cnt_ref, rb_ref, ea_ref, eb_ref, va_ref, vb_ref, *, sub):
    nk = PEER_NKEYS
    iota = lax.broadcasted_iota(jnp.int32, (nk, LANES), 0).astype(F32)
    pos = pos_ref[...]
    valid = pos < 1e8

    def pick(rem, r, v_ref):
        m = jnp.max(rem, axis=0, keepdims=True)
        idx = jnp.min(jnp.where(rem == m, iota, float(nk)), axis=0, keepdims=True)
        v_ref[pl.ds(r, 1), :] = m
        return jnp.where(iota == idx, -_TAKEN * (1.0 + jnp.asarray(r, F32) * (1.0 / 64)), rem)

    def rank_of(rem):
        return jnp.where(rem <= -_TAKEN, (rem * (-1.0 / _TAKEN) - 1.0) * 64.0, float(PEER_TOPK))

    grp = min(SEL_GROUP, sub)
    for s0 in range(0, sub, grp):
        lss = [slice((s0 + g) * LANES, (s0 + g + 1) * LANES) for g in range(grp)]
        sas = [sa_ref[:, ls] for ls in lss]
        sbs = [sb_ref[:, ls] for ls in lss]

        def body1(r, c):
            out = []
            for g in range(grp):
                out.append(pick(c[2 * g], r, va_ref.at[g]))
                out.append(pick(c[2 * g + 1], r, vb_ref.at[g]))
            return tuple(out)

        rems = lax.fori_loop(0, PEER_TOPK, body1, tuple(x for g in range(grp) for x in (sas[g], sbs[g])))
        vas = [va_ref[g] for g in range(grp)]
        vbs = [vb_ref[g] for g in range(grp)]
        cands, smaxs = [], []
        for g in range(grp):
            va, vb = vas[g], vbs[g]
            blocks = [va[0:1, :] + vb]
            for i in range(1, 8):
                blocks.append(va[i:i + 1, :] + vb[0:8, :])
            blocks.append(va[8:16, :] + vb[0:1, :])
            cands.append(jnp.where(valid, jnp.concatenate(blocks, axis=0), NEG_INF))
            smaxs.append(va[0:1, :] + vb[0:1, :])

        def body2(r, carry):
            out = []
            for g in range(grp):
                cand, taken, z = carry[3 * g:3 * g + 3]
                m = jnp.max(cand, axis=0, keepdims=True)
                p = jnp.min(jnp.where(cand == m, pos, 1e9), axis=0, keepdims=True)
                sel = pos == p
                out += [jnp.where(sel, NEG_INF, cand), jnp.where(sel, 1.0, taken), z + jnp.exp(m - smaxs[g])]
            return tuple(out)

        init = []
        for g in range(grp):
            init += [cands[g], jnp.zeros_like(cands[g]), jnp.zeros_like(smaxs[g])]
        fin = lax.fori_loop(0, PEER_TOPK, body2, tuple(init))

        for g in range(grp):
            ls, sa, sb, va, vb = lss[g], sas[g], sbs[g], vas[g], vbs[g]
            taken, z = fin[3 * g + 1], fin[3 * g + 2]
            rank_a = rank_of(rems[2 * g])
            rank_b = rank_of(rems[2 * g + 1])
            cnt = jnp.zeros_like(sa)
            for i in range(PEER_TOPK):
                if i == 0:
                    c = jnp.sum(taken[0:16, :], axis=0, keepdims=True)
                elif i < 8:
                    c = jnp.sum(taken[16 + 8 * (i - 1):16 + 8 * i, :], axis=0, keepdims=True)
                else:
                    c = taken[72 + (i - 8):73 + (i - 8), :]
                cnt = jnp.where(rank_a == float(i), c, cnt)
            cnt_ref[:, ls] = cnt
            rb_ref[:, ls] = rank_b.astype(BF16)
            ea_ref[:, ls] = jnp.exp(sa - va[0:1, :]) * (2.0 ** -0.5 / z)
            eb_ref[:, ls] = jnp.exp(sb - vb[0:1, :]).astype(BF16)


def _peer_select(pos, sat, sbt, *, ts):
    nh, nk, n = sat.shape
    spec = pl.BlockSpec((None, nk, ts), lambda t, h: (h, 0, t))
    shp = jax.ShapeDtypeStruct((nh, nk, n), F32)
    shp_bf = jax.ShapeDtypeStruct((nh, nk, n), BF16)
    return pl.pallas_call(
        functools.partial(_sel_kernel, sub=ts // LANES),
        out_shape=(shp, shp_bf, shp, shp_bf),
        grid=(n // ts, nh),
        in_specs=[pl.BlockSpec(pos.shape, lambda t, h: (0, 0)), spec, spec],
        out_specs=(spec, spec, spec, spec),
        scratch_shapes=[pltpu.VMEM((SEL_GROUP, PEER_TOPK, LANES), F32),
                        pltpu.VMEM((SEL_GROUP, PEER_TOPK, LANES), F32)],
        compiler_params=_params("parallel", "parallel"),
        name="peer_select",
    )(pos, sat, sbt)


def _peer_kernel(hf_ref, u_ref, vt_ref, cnt_ref, rb_ref, ea_ref, eb_ref, x1_ref, gf_ref, y_ref, acc_ref,
                 *, na, tc):
    e = pl.program_id(1)
    nk = PEER_NKEYS
    tm = hf_ref.shape[0]
    chunks = [slice(c * tc, (c + 1) * tc) for c in range(tm // tc)]

    @pl.when(e == 0)
    def _():
        acc_ref[...] = jnp.zeros_like(acc_ref)

    def row(ref, h, a, cs):
        r = jnp.broadcast_to(ref[h, pl.ds(a, 1), cs], (_BF16_ROWS, tc)).astype(BF16)
        return jnp.tile(r, (nk // _BF16_ROWS, 1))

    acts = [_nt(u_ref[...], hf_ref[cs, :]) for cs in chunks]
    zero = jnp.zeros((nk, tc), BF16)
    for cs, act in zip(chunks, acts):
        gel = (act * (1.0 + lax.erf(act))).astype(BF16)
        slabs = []
        for ai in range(na):
            a = e * na + ai
            g = zero
            for h in range(PEER_HEADS):
                taken = rb_ref[h, :, cs] < row(cnt_ref, h, a, cs)
                g = g + jnp.where(taken, eb_ref[h, :, cs], zero) * row(ea_ref, h, a, cs)
            slabs.append(g * gel[ai * nk:(ai + 1) * nk, :])
        acc_ref[:, cs] += _nn(vt_ref[...], jnp.concatenate(slabs, axis=0))

    @pl.when(e == pl.num_programs(1) - 1)
    def _():
        y_ref[...] = x1_ref[...] + gf_ref[...] * acc_ref[...].T


def _peer_experts(hf, u_bf, vt_bf, cnt, rb, ea, eb, x1, gf3, *, tm, eb_sz, tiles_per_mod):
    n, d = hf.shape
    nblk = u_bf.shape[0] // eb_sz
    r = gf3.shape[1]
    na = eb_sz // PEER_NKEYS
    sel = pl.BlockSpec((PEER_HEADS, PEER_NKEYS, tm), lambda t, e: (0, 0, t))
    return pl.pallas_call(
        functools.partial(_peer_kernel, na=na, tc=min(PEER_TC, tm)),
        out_shape=jax.ShapeDtypeStruct((n, d), F32),
        grid=(n // tm, nblk),
        in_specs=[pl.BlockSpec((tm, d), lambda t, e: (t, 0)),
                  pl.BlockSpec((eb_sz, d), lambda t, e: (e, 0)),
                  pl.BlockSpec((d, eb_sz), lambda t, e: (0, e)),
                  sel, sel, sel, sel,
                  pl.BlockSpec((tm, d), lambda t, e: (t, 0)),
                  pl.BlockSpec((None, r, d), lambda t, e: (t // tiles_per_mod, 0, 0))],
        out_specs=pl.BlockSpec((tm, d), lambda t, e: (t, 0)),
        scratch_shapes=[pltpu.VMEM((d, tm), F32)],
        compiler_params=_params("parallel", "arbitrary"),
        name="peer_experts",
    )(hf, u_bf, vt_bf, cnt, rb, ea, eb, x1, gf3)


IN_TM = 256
ML_CHUNK = 256
ML_CHUNK_DEC = 128
SB_DEC_PAGES = 4
SB_PROMPT_HEADS = 4
SEL_TS = 512
SEL_GROUP = 2
PEER_TM = 1024
PEER_EB = 512
PEER_TC = 512


def _layer_weights(w_in, b_gates, qnorm_g, knorm_g, w_out, w_query, sub_keys_a, sub_keys_b, expert_u,
                   expert_v):
    w_m = ML_HEADS * ML_DH
    ng = 2 * ML_HEADS
    g0 = 4 * w_m
    d = w_in.shape[0]
    w_main = jnp.concatenate([w_in[:, :g0], w_in[:, g0 + ng:]], axis=1).astype(BF16)
    wg = w_in[:, g0:g0 + ng]
    w_gate = jnp.pad(wg, ((0, 0), (0, LANES - ng))).astype(BF16)
    w_gate_t = wg.T.astype(BF16)
    bg_row = jnp.pad(b_gates, (0, LANES - ng)).reshape(1, LANES)
    bg_col = b_gates.reshape(ng, 1)
    head = np.arange(SB_HEADS * SB_DH) // SB_DH
    bd = jnp.asarray(head[:, None] == head[None, :], BF16)
    return dict(w_main=w_main, w_gate=w_gate, w_gate_t=w_gate_t, bg_row=bg_row, bg_col=bg_col,
                gq=qnorm_g.reshape(1, -1), gk=knorm_g.reshape(1, -1), bd=bd,
                w_out=w_out.astype(BF16), w_query=w_query.astype(BF16),
                ka=sub_keys_a.astype(BF16), kb=sub_keys_b.astype(BF16),
                u=(expert_u * 2.0 ** -0.5).astype(BF16), vt=expert_v.T.astype(BF16), d=d)


def kernel(x_prompt, x_sample, cache_sb_k, cache_sb_v, state_mlstm_C, state_mlstm_n, state_mlstm_m,
           page_table, c_prompt, c_sample, w_ada, b_ada, norm_mix_g, norm_ffn_g, w_in, b_gates, qnorm_g,
           knorm_g, sb_bias, mlstm_head_g, w_out, w_query, sub_keys_a, sub_keys_b, expert_u, expert_v):
    depth = w_ada.shape[0]
    assert depth == 1, "single-layer trunk"
    bp, tp, d = x_prompt.shape
    bs, ts, _ = x_sample.shape
    w_m = ML_HEADS * ML_DH
    w_s = SB_HEADS * SB_DH
    psz = cache_sb_k.shape[2]

    lw = _layer_weights(w_in[0], b_gates[0], qnorm_g[0], knorm_g[0], w_out[0], w_query[0], sub_keys_a[0],
                        sub_keys_b[0], expert_u[0], expert_v[0])
    g_mix = norm_mix_g[0].reshape(1, d)
    g_ffn = norm_ffn_g[0].reshape(1, d)
    gh = mlstm_head_g[0].reshape(1, w_m)
    bias = sb_bias[0].astype(F32)
    pos = jnp.asarray(_cand_tables())

    nrow = bp + bs
    pad = (-nrow) % 8
    c_all = jnp.pad(jnp.concatenate([c_prompt, c_sample], axis=0).astype(F32), ((0, pad), (0, 0)))
    mod = _ada(c_all, w_ada[0].astype(BF16), b_ada[0].reshape(1, -1))
    mods = [mod[:, i * d:(i + 1) * d] for i in range(6)]

    def group_mods(lo, hi, per_token):
        out = []
        for m in mods:
            m = m[lo:hi]
            if per_token:
                m = jnp.repeat(m, ts, axis=0).reshape(1, (hi - lo) * ts, d)
            else:
                m = m.reshape(hi - lo, 1, d)
            out.append(m)
        return out

    def peer(hf, x1, gf3, sat, sbt, n_tok, tm, tpm):
        sel_ts = min(SEL_TS, n_tok)
        cnt, rb, ea, eb = _peer_select(pos, sat, sbt, ts=sel_ts)
        return _peer_experts(hf, lw["u"], lw["vt"], cnt, rb, ea, eb, x1, gf3, tm=tm, eb_sz=PEER_EB,
                             tiles_per_mod=tpm)

    sh_a, sc_a, ga, sh_f, sc_f, gf = group_mods(0, bp, False)
    xp = x_prompt.reshape(bp * tp, d).astype(F32)
    tpm = tp // IN_TM
    (mq, mk, mv, og, gcol, grow, qn, kbf, ks_p, vs_p, vt) = _in_proj(
        xp, sc_a, sh_a, g_mix, lw["w_main"], lw["w_gate"], lw["w_gate_t"], lw["bg_row"], lw["bg_col"],
        lw["gq"], lw["gk"], lw["bd"], tm=IN_TM, tiles_per_mod=tpm)
    zc = jnp.zeros((bp, ML_HEADS, ML_DH, ML_DH), F32)
    zn = jnp.zeros((bp, ML_HEADS, ML_DH), F32)
    hm, c_p, n_p, m_p = _mlstm(mq, mk, mv, og, gcol, grow, zc, zn, zn, gh, nb=bp, chunk=ML_CHUNK)
    hs = _sb_prompt(bias, qn, kbf, vt, nb=bp, blk=IN_TM, heads=SB_PROMPT_HEADS)
    x1, hf, sat, sbt = _out_proj(xp, hm, hs, ga, sc_f, sh_f, g_ffn, lw["w_out"], lw["w_query"], lw["ka"],
                                 lw["kb"], tm=IN_TM, tiles_per_mod=tpm)
    y_p = peer(hf, x1, gf, sat, sbt, bp * tp, PEER_TM, tp // PEER_TM)

    sh_a, sc_a, ga, sh_f, sc_f, gf = group_mods(bp, bp + bs, True)
    ns = bs * ts
    xs = x_sample.reshape(ns, d).astype(F32)
    (mq, mk, mv, og, gcol, grow, qn, kbf, ks_s, vs_s, _) = _in_proj(
        xs, sc_a, sh_a, g_mix, lw["w_main"], lw["w_gate"], lw["w_gate_t"], lw["bg_row"], lw["bg_col"],
        lw["gq"], lw["gk"], lw["bd"], tm=ns, tiles_per_mod=1)

    lc = ML_CHUNK_DEC

    def pad_seq(a, fill=0.0):
        a = a.reshape(bs, ts, a.shape[-1])
        return jnp.pad(a, ((0, 0), (0, lc - ts), (0, 0)), constant_values=fill).reshape(bs * lc, -1)

    gcol_fill = jnp.where(jnp.arange(LANES) < ML_HEADS, -1e30, 0.0).astype(F32)
    gcol_p = gcol.reshape(bs, ts, LANES)
    gcol_p = jnp.concatenate(
        [gcol_p, jnp.broadcast_to(gcol_fill, (bs, lc - ts, LANES))], axis=1).reshape(bs * lc, LANES)
    grow_fill = jnp.where(jnp.arange(8) < ML_HEADS, -1e30, 0.0).astype(F32)
    grow_p = grow.reshape(8, bs, ts)
    grow_p = jnp.concatenate(
        [grow_p, jnp.broadcast_to(grow_fill[:, None, None], (8, bs, lc - ts))], axis=2).reshape(8, bs * lc)
    m0 = jnp.broadcast_to(state_mlstm_m[0].astype(F32)[:, :, None], (bs, ML_HEADS, ML_DH))
    hm_pad, c_s, n_s, m_s = _mlstm(pad_seq(mq), pad_seq(mk), pad_seq(mv), pad_seq(og), gcol_p, grow_p,
                                   state_mlstm_C[0].astype(F32), state_mlstm_n[0].astype(F32), m0, gh,
                                   nb=bs, chunk=lc)
    hm = hm_pad.reshape(bs, lc, w_m)[:, :ts].reshape(ns, w_m)

    lane_head = jnp.arange(w_s) // SB_DH
    q3 = qn.reshape(bs, 1, ts, w_s)
    qbd = jnp.where(lane_head[None, None, None, :] == jnp.arange(SB_HEADS)[None, :, None, None], q3,
                    jnp.zeros_like(q3)).reshape(bs, SB_HEADS * ts, w_s)
    bias_rep = jnp.broadcast_to(jnp.repeat(bias * LOG2E, ts)[:, None], (SB_HEADS * ts, psz))
    knew = jnp.pad(kbf.reshape(bs, ts, w_s), ((0, 0), (0, psz - ts), (0, 0)))
    vnew = jnp.pad(vs_s.astype(BF16).reshape(bs, ts, w_s), ((0, 0), (0, psz - ts), (0, 0)))
    ck = cache_sb_k[0].reshape(cache_sb_k.shape[1], psz, w_s)
    cv = cache_sb_v[0].reshape(cache_sb_v.shape[1], psz, w_s)
    hs = _sb_decode(page_table.astype(jnp.int32), qbd, bias_rep, knew, vnew, ck, cv, pages=SB_DEC_PAGES,
                    tq=ts).reshape(ns, w_s)
    x1, hf, sat, sbt = _out_proj(xs, hm, hs, ga, sc_f, sh_f, g_ffn, lw["w_out"], lw["w_query"], lw["ka"],
                                 lw["kb"], tm=ns, tiles_per_mod=1)
    y_s = peer(hf, x1, gf, sat, sbt, ns, ns, 1)

    dt_k, dt_v = cache_sb_k.dtype, cache_sb_v.dtype
    return (y_p.reshape(bp, tp, d).astype(x_prompt.dtype),
            y_s.reshape(bs, ts, d).astype(x_sample.dtype),
            ks_p.reshape(1, bp, tp, SB_HEADS, SB_DH).astype(dt_k),
            vs_p.reshape(1, bp, tp, SB_HEADS, SB_DH).astype(dt_v),
            c_p[None], n_p[None], m_p[None, :, :, 0],
            ks_s.reshape(1, bs, ts, SB_HEADS, SB_DH).astype(dt_k),
            vs_s.reshape(1, bs, ts, SB_HEADS, SB_DH).astype(dt_v),
            c_s[None], n_s[None], m_s[None, :, :, 0])
```

```python
import functools

import numpy as np
import jax
import jax.numpy as jnp
from jax import lax
from jax.experimental import pallas as pl
from jax.experimental.pallas import tpu as pltpu

F32 = jnp.float32
BF16 = jnp.bfloat16

EPS = 1e-6
GATE_SOFTCAP = 15.0
SB_HEADS = 8
SB_DH = 64
ML_HEADS = 4
ML_DH = 128
PEER_HEADS = 8
PEER_NKEYS = 128
PEER_TOPK = 16

LOG2E = 1.4426950408889634
LANES = 128
_BF16_ROWS = 16
VMEM_LIMIT = 56 * 1024 * 1024

NEG_INF = float("-inf")


def _nn(a, b):
    return jnp.dot(a, b, preferred_element_type=F32)


def _nt(a, b):
    return lax.dot_general(a, b, (((1,), (1,)), ((), ())), preferred_element_type=F32)


def _split(x):
    hi = x.astype(BF16)
    lo = (x - hi.astype(F32)).astype(BF16)
    return hi, lo


def _softplus(z):
    return jnp.maximum(z, 0.0) + jnp.log1p(jnp.exp(-jnp.abs(z)))


def _log_sigmoid(z):
    return -_softplus(-z)


def _softcap(a):
    return GATE_SOFTCAP * jnp.tanh(a / GATE_SOFTCAP)


def _params(*sem):
    return pltpu.CompilerParams(dimension_semantics=sem, vmem_limit_bytes=VMEM_LIMIT)


def _ada_kernel(c_ref, w_ref, b_ref, o_ref):
    c = c_ref[...]
    a = (c * jax.nn.sigmoid(c)).astype(BF16)
    o_ref[...] = _nn(a, w_ref[...]) + b_ref[...]


def _ada(c_all, w_ada, b_ada):
    rows, d = c_all.shape
    n_out = w_ada.shape[1]
    return pl.pallas_call(
        _ada_kernel,
        out_shape=jax.ShapeDtypeStruct((rows, n_out), F32),
        grid=(n_out // d,),
        in_specs=[pl.BlockSpec((rows, d), lambda j: (0, 0)),
                  pl.BlockSpec((d, d), lambda j: (0, j)),
                  pl.BlockSpec((1, d), lambda j: (0, j))],
        out_specs=pl.BlockSpec((rows, d), lambda j: (0, j)),
        compiler_params=_params("parallel"),
        name="ada_mod",
    )(c_all, w_ada, b_ada)


def _in_kernel(x_ref, sc_ref, sh_ref, g_ref, w_ref, wg_ref, wgt_ref, bgr_ref, bgc_ref, gq_ref, gk_ref,
               bd_ref, mq_ref, mk_ref, mv_ref, og_ref, gcol_ref, grow_ref, qn_ref, kbf_ref, ks_ref,
               vs_ref, vt_ref):
    w_m = ML_HEADS * ML_DH
    x = x_ref[...]
    h = x * lax.rsqrt(jnp.mean(x * x, axis=-1, keepdims=True) + EPS) * g_ref[...]
    h = h * (1.0 + sc_ref[...]) + sh_ref[...]
    hb = h.astype(BF16)

    def seg(k):
        return _nn(hb, w_ref[:, k * w_m:(k + 1) * w_m])

    mq_ref[...] = seg(0).astype(BF16)
    mk_ref[...] = (seg(1) * ML_DH ** -0.5).astype(BF16)
    mv_ref[...] = seg(2).astype(BF16)
    og_ref[...] = jax.nn.sigmoid(seg(3)).astype(BF16)

    gc = _softcap(_nn(hb, wg_ref[...]) + bgr_ref[...])
    lane = lax.broadcasted_iota(jnp.int32, gc.shape, 1)
    gcol_ref[...] = jnp.where(lane < ML_HEADS, gc, _log_sigmoid(gc))
    gr = _softcap(_nt(wgt_ref[...], hb) + bgc_ref[...])
    row = lax.broadcasted_iota(jnp.int32, gr.shape, 0)
    grow_ref[...] = jnp.where(row < ML_HEADS, gr, _log_sigmoid(gr))

    def head_norm(s, g):
        hi, lo = _split(s * s)
        ss = _nn(hi, bd_ref[...]) + _nn(lo, bd_ref[...])
        return s * lax.rsqrt(ss * (1.0 / SB_DH) + EPS) * g

    qn_ref[...] = (head_norm(seg(4), gq_ref[...]) * (SB_DH ** -0.5 * LOG2E)).astype(BF16)
    kn = head_norm(seg(5), gk_ref[...])
    ks_ref[...] = kn
    kbf_ref[...] = kn.astype(BF16)
    sv = seg(6)
    vs_ref[...] = sv
    for p in range(w_m // LANES):
        vt_ref[p * LANES:(p + 1) * LANES, :] = sv[:, p * LANES:(p + 1) * LANES].T.astype(BF16)


def _in_proj(x2d, sc3, sh3, g_mix, w_main, w_gate, w_gate_t, bg_row, bg_col, gq, gk, bd, *, tm,
             tiles_per_mod):
    n, d = x2d.shape
    nt = n // tm
    r = sc3.shape[1]
    w = ML_HEADS * ML_DH
    const = lambda t: (0, 0)
    mod_map = lambda t: (t // tiles_per_mod, 0, 0)
    tile = lambda t: (t, 0)
    tok = lambda dt: jax.ShapeDtypeStruct((n, w), dt)
    return pl.pallas_call(
        _in_kernel,
        out_shape=(tok(BF16), tok(BF16), tok(BF16), tok(BF16),
                   jax.ShapeDtypeStruct((n, LANES), F32), jax.ShapeDtypeStruct((8, n), F32),
                   tok(BF16), tok(BF16), tok(F32), tok(F32),
                   jax.ShapeDtypeStruct((nt, w, tm), BF16)),
        grid=(nt,),
        in_specs=[pl.BlockSpec((tm, d), tile),
                  pl.BlockSpec((None, r, d), mod_map),
                  pl.BlockSpec((None, r, d), mod_map),
                  pl.BlockSpec((1, d), const),
                  pl.BlockSpec(w_main.shape, const),
                  pl.BlockSpec(w_gate.shape, const),
                  pl.BlockSpec(w_gate_t.shape, const),
                  pl.BlockSpec(bg_row.shape, const),
                  pl.BlockSpec(bg_col.shape, const),
                  pl.BlockSpec((1, w), const),
                  pl.BlockSpec((1, w), const),
                  pl.BlockSpec((w, w), const)],
        out_specs=(pl.BlockSpec((tm, w), tile), pl.BlockSpec((tm, w), tile), pl.BlockSpec((tm, w), tile),
                   pl.BlockSpec((tm, w), tile), pl.BlockSpec((tm, LANES), tile),
                   pl.BlockSpec((8, tm), lambda t: (0, t)),
                   pl.BlockSpec((tm, w), tile), pl.BlockSpec((tm, w), tile), pl.BlockSpec((tm, w), tile),
                   pl.BlockSpec((tm, w), tile), pl.BlockSpec((None, w, tm), lambda t: (t, 0, 0))),
        compiler_params=_params("parallel"),
        name="in_proj",
    )(x2d, sc3, sh3, g_mix, w_main, w_gate, w_gate_t, bg_row, bg_col, gq, gk, bd)


def _mlstm_kernel(q_ref, k_ref, v_ref, og_ref, gcol_ref, grow_ref, c0_ref, n0_ref, m0_ref, gh_ref,
                  hm_ref, c_ref, n_ref, m_ref, *, chunk):
    L = chunk

    @pl.when(pl.program_id(1) == 0)
    def _():
        c_ref[...] = c0_ref[...]
        n_ref[...] = n0_ref[...]
        m_ref[...] = m0_ref[...]

    t_i = lax.broadcasted_iota(jnp.int32, (L, L), 0)
    s_i = lax.broadcasted_iota(jnp.int32, (L, L), 1)
    causal = s_i <= t_i
    tri_lo = causal.astype(BF16)
    tri_up = (t_i <= s_i).astype(BF16)

    gcol = gcol_ref[...]
    grow = grow_ref[...]
    chi, clo = _split(gcol)
    bc = _nn(tri_lo, chi) + _nn(tri_lo, clo)
    rhi, rlo = _split(grow)
    br = _nn(rhi, tri_up) + _nn(rlo, tri_up)

    for h in range(ML_HEADS):
        sl = slice(h * ML_DH, (h + 1) * ML_DH)
        icol = gcol[:, h:h + 1]
        bcol = bc[:, ML_HEADS + h:ML_HEADS + h + 1]
        irow = grow[h:h + 1, :]
        brow = br[ML_HEADS + h:ML_HEADS + h + 1, :]
        m_prev = m_ref[h:h + 1, 0:1]

        dmat = jnp.where(causal, bcol - brow + irow, NEG_INF)
        inter = bcol + m_prev
        mt = jnp.maximum(inter, jnp.max(dmat, axis=1, keepdims=True))
        w_intra = jnp.exp(dmat - mt)
        w_inter = jnp.exp(inter - mt)

        q = q_ref[:, sl]
        k = k_ref[:, sl]
        v = v_ref[:, sl]
        sw = _nt(q, k) * w_intra
        n_row = n_ref[h:h + 1, :]
        num = w_inter * _nt(q, c_ref[h].astype(BF16)) + _nn(sw.astype(BF16), v)
        qn = jnp.sum(q.astype(F32) * n_row, axis=1, keepdims=True)
        dot = w_inter * qn + jnp.sum(sw, axis=1, keepdims=True)
        hh = num / jnp.maximum(jnp.abs(dot), jnp.exp(-mt))
        hn = hh * lax.rsqrt(jnp.mean(hh * hh, axis=1, keepdims=True) + EPS) * gh_ref[:, sl]
        hm_ref[:, sl] = (hn * og_ref[:, sl].astype(F32)).astype(BF16)

        m_last = mt[L - 1:L, :]
        b_last = bcol[L - 1:L, :]
        decay = jnp.exp(b_last - bcol + icol - m_last)
        carry = jnp.exp(b_last + m_prev - m_last)
        vd = (v.astype(F32) * decay).T.astype(BF16)
        c_ref[h] = carry * c_ref[h] + _nn(vd, k)
        n_ref[h:h + 1, :] = carry * n_row + jnp.sum(decay * k.astype(F32), axis=0, keepdims=True)
        m_ref[h:h + 1, :] = jnp.broadcast_to(m_last, (1, ML_DH))


def _mlstm(mq, mk, mv, og, gcol, grow, c0, n0, m0, gh, *, nb, chunk):
    n, w = mq.shape
    nc = n // (nb * chunk)
    blk = lambda b, c: (b * nc + c, 0)
    st4 = lambda b, c: (b, 0, 0, 0)
    st3 = lambda b, c: (b, 0, 0)
    return pl.pallas_call(
        functools.partial(_mlstm_kernel, chunk=chunk),
        out_shape=(jax.ShapeDtypeStruct((n, w), BF16),
                   jax.ShapeDtypeStruct((nb, ML_HEADS, ML_DH, ML_DH), F32),
                   jax.ShapeDtypeStruct((nb, ML_HEADS, ML_DH), F32),
                   jax.ShapeDtypeStruct((nb, ML_HEADS, ML_DH), F32)),
        grid=(nb, nc),
        in_specs=[pl.BlockSpec((chunk, w), blk), pl.BlockSpec((chunk, w), blk),
                  pl.BlockSpec((chunk, w), blk), pl.BlockSpec((chunk, w), blk),
                  pl.BlockSpec((chunk, LANES), blk),
                  pl.BlockSpec((8, chunk), lambda b, c: (0, b * nc + c)),
                  pl.BlockSpec((None, ML_HEADS, ML_DH, ML_DH), st4),
                  pl.BlockSpec((None, ML_HEADS, ML_DH), st3),
                  pl.BlockSpec((None, ML_HEADS, ML_DH), st3),
                  pl.BlockSpec((1, w), lambda b, c: (0, 0))],
        out_specs=(pl.BlockSpec((chunk, w), blk),
                   pl.BlockSpec((None, ML_HEADS, ML_DH, ML_DH), st4),
                   pl.BlockSpec((None, ML_HEADS, ML_DH), st3),
                   pl.BlockSpec((None, ML_HEADS, ML_DH), st3)),
        compiler_params=_params("parallel", "arbitrary"),
        name="mlstm",
    )(mq, mk, mv, og, gcol, grow, c0, n0, m0, gh)


def _sb_kernel(bias_ref, q_ref, k_ref, vt_ref, o_ref, acc_ref, *, blk, heads):
    grp0 = pl.program_id(1)
    qi = pl.program_id(2)
    lane = lax.broadcasted_iota(jnp.int32, (blk, LANES), 1)
    r_i = lax.broadcasted_iota(jnp.int32, (blk, blk), 0)
    c_i = lax.broadcasted_iota(jnp.int32, (blk, blk), 1)
    suffix = (c_i >= r_i).astype(BF16)
    visible = r_i < c_i

    q_neg, bias_neg = [], []
    for hh in range(heads):
        grp, half = divmod(hh, 2)
        q2 = q_ref[:, grp * LANES:(grp + 1) * LANES]
        keep = (lane >= half * SB_DH) & (lane < (half + 1) * SB_DH)
        q_neg.append(jnp.where(keep, -q2, jnp.zeros_like(q2)))
        bias_neg.append(-LOG2E * bias_ref[heads * grp0 + hh])
    acc_ref[...] = jnp.zeros_like(acc_ref)

    def step(js, runs, masked):
        hs = range(heads)
        xs = range(len(js))
        starts = [pl.multiple_of(j * blk, blk) for j in js]
        gl = [slice((hh // 2) * LANES, (hh // 2 + 1) * LANES) for hh in hs]
        zn = [[_nt(k_ref[pl.ds(starts[x], blk), gl[hh]], q_neg[hh]) + bias_neg[hh] for hh in hs]
              for x in xs]
        lm = [[jnp.minimum(z, 0.0) - jnp.log(1.0 + jnp.exp2(-jnp.abs(z))) * LOG2E for z in row]
              for row in zn]
        if masked:
            lm = [[jnp.where(visible, v, 0.0) for v in row] for row in lm]
        parts = [[_split(v) for v in row] for row in lm]
        p = [[_nn(suffix, hi) + _nn(suffix, lo) for hi, lo in row] for row in parts]
        for x in xs:
            a = [jnp.exp2((runs[hh] + p[x][hh]) - zn[x][hh]) for hh in hs]
            if masked:
                a = [jnp.where(visible, v, 0.0) for v in a]
            for hh in hs:
                acc_ref[hh] += _nn(vt_ref[js[x], gl[hh], :], a[hh].astype(BF16))
            runs = tuple(runs[hh] + p[x][hh][0:1, :] for hh in hs)
        return runs

    runs = step([qi], tuple(jnp.zeros((1, blk), F32) for _ in range(heads)), True)
    odd = qi % 2
    runs = lax.cond(odd == 1, lambda r: step([qi - 1], r, False), lambda r: r, runs)
    base = qi - 1 - odd
    lax.fori_loop(0, (qi - odd) // 2, lambda t, r: step([base - 2 * t, base - 2 * t - 1], r, False), runs)
    for grp in range(heads // 2):
        both = jnp.concatenate([acc_ref[2 * grp, 0:SB_DH, :], acc_ref[2 * grp + 1, SB_DH:, :]], axis=0)
        o_ref[:, grp * LANES:(grp + 1) * LANES] = both.T.astype(BF16)


def _sb_prompt(sb_bias, qn, kbf, vt, *, nb, blk, heads):
    n, w = qn.shape
    t = n // nb
    nq = t // blk
    wb = heads * SB_DH
    return pl.pallas_call(
        functools.partial(_sb_kernel, blk=blk, heads=heads),
        out_shape=jax.ShapeDtypeStruct((n, w), BF16),
        grid=(nb, w // wb, nq),
        in_specs=[pl.BlockSpec(memory_space=pltpu.SMEM),
                  pl.BlockSpec((blk, wb), lambda b, p, i: (b * nq + i, p)),
                  pl.BlockSpec((t, wb), lambda b, p, i: (b, p)),
                  pl.BlockSpec((nq, wb, blk), lambda b, p, i: (b, p, 0))],
        out_specs=pl.BlockSpec((blk, wb), lambda b, p, i: (b * nq + i, p)),
        scratch_shapes=[pltpu.VMEM((heads, LANES, blk), F32)],
        compiler_params=_params("parallel", "parallel", "arbitrary"),
        name="sb_prompt",
    )(sb_bias, qn, kbf, vt)


def _sbd_kernel(pt_ref, q_ref, bias_ref, knew_ref, vnew_ref, *refs, pages, tq, new_keys):
    k_refs = refs[:pages]
    v_refs = refs[pages:2 * pages]
    o_ref = refs[2 * pages]
    acc_ref, run_ref, fold_ref = refs[2 * pages + 1:]
    j = pl.program_id(1)
    nh = SB_HEADS
    cols = nh * tq
    q = q_ref[...]
    bias = bias_ref[...]

    def blocks(kfs, vfs, nkeys, visible):
        rows = nkeys * nh
        xs = range(len(kfs))
        own = [lax.broadcasted_iota(jnp.int32, (nkeys, LANES), 1) // tq == h for h in range(nh)]
        r_i = lax.broadcasted_iota(jnp.int32, (nkeys, nkeys), 0)
        c_i = lax.broadcasted_iota(jnp.int32, (nkeys, nkeys), 1)
        suffix = (c_i >= r_i).astype(BF16)
        for x in xs:
            fold_ref[x, 0:rows, 0:cols] = _nt(kfs[x], q)
        zs, lms = [], []
        for x in xs:
            z = bias
            for h in range(nh):
                z = z + jnp.where(own[h], fold_ref[x, pl.ds(h, nkeys, stride=nh), :], 0.0)
            lm = -(jnp.maximum(z, 0.0) + jnp.log(1.0 + jnp.exp2(-jnp.abs(z))) * LOG2E)
            if visible is not None:
                lm = jnp.where(visible, lm, 0.0)
            zs.append(z)
            lms.append(_split(lm))
        ps = [_nn(suffix, hi) + _nn(suffix, lo) for hi, lo in lms]
        run = run_ref[...]
        afs = []
        for x in xs:
            a = jnp.exp2(zs[x] + run + ps[x])
            if visible is not None:
                a = jnp.where(visible, a, 0.0)
            for h in range(nh):
                fold_ref[x, pl.ds(h, nkeys, stride=nh), :] = jnp.where(own[h], a, 0.0)
            afs.append(fold_ref[x, 0:rows, 0:cols].astype(BF16))
            run = run + ps[x][0:1, :]
        run_ref[...] = run
        upd = None
        for x in xs:
            t = lax.dot_general(vfs[x], afs[x], (((0,), (0,)), ((), ())), preferred_element_type=F32)
            upd = t if upd is None else upd + t
        acc_ref[...] += upd

    @pl.when(j == 0)
    def _():
        acc_ref[...] = jnp.zeros_like(acc_ref)
        run_ref[...] = jnp.zeros_like(run_ref)
        fold_ref[...] = jnp.zeros_like(fold_ref)
        k_pos = lax.broadcasted_iota(jnp.int32, (new_keys, LANES), 0)
        q_pos = lax.broadcasted_iota(jnp.int32, (new_keys, LANES), 1) % tq
        blocks([knew_ref[...]], [vnew_ref[...]], new_keys, k_pos < q_pos)

    blocks([r[...].astype(BF16) for r in k_refs], [r[...].astype(BF16) for r in v_refs],
           k_refs[0].shape[0] // nh, None)

    @pl.when(j == pl.num_programs(1) - 1)
    def _():
        o_ref[...] = acc_ref[...]


def _sb_decode(page_table, q_all, bias_row, knew, vnew, cache_k, cache_v, *, pages, tq, new_keys):
    nb, npg = page_table.shape
    prow, dh = cache_k.shape[1:]
    cols = SB_HEADS * tq
    steps = npg // pages

    def page_map(p):
        return lambda b, j, pt: (pt[b, npg - 1 - (j * pages + p)], 0, 0)

    seq3 = lambda b, j, pt: (b, 0, 0)
    const = lambda b, j, pt: (0, 0)
    grid_spec = pltpu.PrefetchScalarGridSpec(
        num_scalar_prefetch=1,
        grid=(nb, steps),
        in_specs=[pl.BlockSpec((None, cols, dh), seq3),
                  pl.BlockSpec((1, LANES), const),
                  pl.BlockSpec((None, new_keys * SB_HEADS, dh), seq3),
                  pl.BlockSpec((None, new_keys * SB_HEADS, dh), seq3)]
                 + [pl.BlockSpec((None, prow, dh), page_map(p)) for p in range(pages)]
                 + [pl.BlockSpec((None, prow, dh), page_map(p)) for p in range(pages)],
        out_specs=pl.BlockSpec((None, dh, cols), seq3),
        scratch_shapes=[pltpu.VMEM((dh, cols), F32), pltpu.VMEM((1, LANES), F32),
                        pltpu.VMEM((pages, prow, LANES), F32)],
    )
    return pl.pallas_call(
        functools.partial(_sbd_kernel, pages=pages, tq=tq, new_keys=new_keys),
        out_shape=jax.ShapeDtypeStruct((nb, dh, cols), F32),
        grid_spec=grid_spec,
        compiler_params=_params("parallel", "arbitrary"),
        name="sb_decode",
    )(page_table, q_all, bias_row, knew, vnew, *([cache_k] * pages), *([cache_v] * pages))


def _out_kernel(x_ref, hm_ref, hs_ref, ga_ref, scf_ref, shf_ref, gf_ref, wo_ref, wq_ref, ka_ref, kb_ref,
                x1_ref, hf_ref, sat_ref, sbt_ref):
    w_m = hm_ref.shape[1]
    mix = _nn(hm_ref[...], wo_ref[0:w_m, :]) + _nn(hs_ref[...], wo_ref[w_m:, :])
    x1 = x_ref[...] + ga_ref[...] * mix
    x1_ref[...] = x1
    hf = x1 * lax.rsqrt(jnp.mean(x1 * x1, axis=-1, keepdims=True) + EPS) * gf_ref[...]
    hf = (hf * (1.0 + scf_ref[...]) + shf_ref[...]).astype(BF16)
    hf_ref[...] = hf
    dk = ka_ref.shape[1]
    for h in range(PEER_HEADS):
        qa = _nn(hf, wq_ref[:, 2 * h * dk:(2 * h + 1) * dk]).astype(BF16)
        qb = _nn(hf, wq_ref[:, (2 * h + 1) * dk:(2 * h + 2) * dk]).astype(BF16)
        sat_ref[h] = _nt(ka_ref[...], qa)
        sbt_ref[h] = _nt(kb_ref[...], qb)


def _out_proj(x2d, hm, hs, ga3, scf3, shf3, g_ffn, w_out, w_query, ka, kb, *, tm, tiles_per_mod):
    n, d = x2d.shape
    nt = n // tm
    r = ga3.shape[1]
    w = hm.shape[1]
    const = lambda t: (0, 0)
    mod_map = lambda t: (t // tiles_per_mod, 0, 0)
    tile = lambda t: (t, 0)
    sc_shape = jax.ShapeDtypeStruct((PEER_HEADS, PEER_NKEYS, n), F32)
    sc_spec = pl.BlockSpec((PEER_HEADS, PEER_NKEYS, tm), lambda t: (0, 0, t))
    return pl.pallas_call(
        _out_kernel,
        out_shape=(jax.ShapeDtypeStruct((n, d), F32), jax.ShapeDtypeStruct((n, d), BF16),
                   sc_shape, sc_shape),
        grid=(nt,),
        in_specs=[pl.BlockSpec((tm, d), tile), pl.BlockSpec((tm, w), tile), pl.BlockSpec((tm, w), tile),
                  pl.BlockSpec((None, r, d), mod_map), pl.BlockSpec((None, r, d), mod_map),
                  pl.BlockSpec((None, r, d), mod_map),
                  pl.BlockSpec((1, d), const), pl.BlockSpec(w_out.shape, const),
                  pl.BlockSpec(w_query.shape, const), pl.BlockSpec(ka.shape, const),
                  pl.BlockSpec(kb.shape, const)],
        out_specs=(pl.BlockSpec((tm, d), tile), pl.BlockSpec((tm, d), tile), sc_spec, sc_spec),
        compiler_params=_params("parallel"),
        name="out_proj",
    )(x2d, hm, hs, ga3, scf3, shf3, g_ffn, w_out, w_query, ka, kb)


_CAND_ROWS = 80
_TAKEN = 2.0 ** 100


def _cand_tables():
    pos = np.full((_CAND_ROWS, LANES), 1e9, np.float32)
    for j in range(PEER_TOPK):
        pos[j] = j
    for i in range(1, 8):
        for j in range(PEER_TOPK // (i + 1)):
            pos[16 + 8 * (i - 1) + j] = i * PEER_TOPK + j
    for r in range(8):
        pos[72 + r] = (8 + r) * PEER_TOPK
    return pos


def _sel_kernel(pos_ref, sa_ref, sb_ref, cnt_ref, rb_ref, ea_ref, eb_ref, va_ref, vb_ref, *, sub):
    nk = PEER_NKEYS
    iota = lax.broadcasted_iota(jnp.int32, (nk, LANES), 0).astype(F32)
    pos = pos_ref[...]
    valid = pos < 1e8

    def pick(rem, r, v_ref):
        m = jnp.max(rem, axis=0, keepdims=True)
        idx = jnp.min(jnp.where(rem == m, iota, float(nk)), axis=0, keepdims=True)
        v_ref[pl.ds(r, 1), :] = m
        return jnp.where(iota == idx, -_TAKEN * (1.0 + jnp.asarray(r, F32) * (1.0 / 64)), rem)

    def rank_of(rem):
        return jnp.where(rem <= -_TAKEN, (rem * (-1.0 / _TAKEN) - 1.0) * 64.0, float(PEER_TOPK))

    grp = min(SEL_GROUP, sub)
    for s0 in range(0, sub, grp):
        lss = [slice((s0 + g) * LANES, (s0 + g + 1) * LANES) for g in range(grp)]
        sas = [sa_ref[:, ls] for ls in lss]
        sbs = [sb_ref[:, ls] for ls in lss]

        def body1(r, c):
            out = []
            for g in range(grp):
                out.append(pick(c[2 * g], r, va_ref.at[g]))
                out.append(pick(c[2 * g + 1], r, vb_ref.at[g]))
            return tuple(out)

        rems = lax.fori_loop(0, PEER_TOPK, body1, tuple(x for g in range(grp) for x in (sas[g], sbs[g])))
        vas = [va_ref[g] for g in range(grp)]
        vbs = [vb_ref[g] for g in range(grp)]
        cands, smaxs = [], []
        for g in range(grp):
            va, vb = vas[g], vbs[g]
            blocks = [va[0:1, :] + vb]
            for i in range(1, 8):
                blocks.append(va[i:i + 1, :] + vb[0:8, :])
            blocks.append(va[8:16, :] + vb[0:1, :])
            cands.append(jnp.where(valid, jnp.concatenate(blocks, axis=0), NEG_INF))
            smaxs.append(va[0:1, :] + vb[0:1, :])

        def body2(r, carry):
            out = []
            for g in range(grp):
                cand, taken, z = carry[3 * g:3 * g + 3]
                m = jnp.max(cand, axis=0, keepdims=True)
                p = jnp.min(jnp.where(cand == m, pos, 1e9), axis=0, keepdims=True)
                sel = pos == p
                out += [jnp.where(sel, NEG_INF, cand), jnp.where(sel, 1.0, taken), z + jnp.exp(m - smaxs[g])]
            return tuple(out)

        init = []
        for g in range(grp):
            init += [cands[g], jnp.zeros_like(cands[g]), jnp.zeros_like(smaxs[g])]
        fin = lax.fori_loop(0, PEER_TOPK, body2, tuple(init))

        for g in range(grp):
            ls, sa, sb, va, vb = lss[g], sas[g], sbs[g], vas[g], vbs[g]
            taken, z = fin[3 * g + 1], fin[3 * g + 2]
            rank_a = rank_of(rems[2 * g])
            rank_b = rank_of(rems[2 * g + 1])
            cnt = jnp.zeros_like(sa)
            for i in range(PEER_TOPK):
                if i == 0:
                    c = jnp.sum(taken[0:16, :], axis=0, keepdims=True)
                elif i < 8:
                    c = jnp.sum(taken[16 + 8 * (i - 1):16 + 8 * i, :], axis=0, keepdims=True)
                else:
                    c = taken[72 + (i - 8):73 + (i - 8), :]
                cnt = jnp.where(rank_a == float(i), c, cnt)
            cnt_ref[:, ls] = cnt
            rb_ref[:, ls] = rank_b.astype(BF16)
            ea_ref[:, ls] = jnp.exp(sa - va[0:1, :]) * (2.0 ** -0.5 / z)
            eb_ref[:, ls] = jnp.exp(sb - vb[0:1, :]).astype(BF16)


def _peer_select(pos, sat, sbt, *, ts):
    nh, nk, n = sat.shape
    spec = pl.BlockSpec((None, nk, ts), lambda t, h: (h, 0, t))
    shp = jax.ShapeDtypeStruct((nh, nk, n), F32)
    shp_bf = jax.ShapeDtypeStruct((nh, nk, n), BF16)
    return pl.pallas_call(
        functools.partial(_sel_kernel, sub=ts // LANES),
        out_shape=(shp, shp_bf, shp, shp_bf),
        grid=(n // ts, nh),
        in_specs=[pl.BlockSpec(pos.shape, lambda t, h: (0, 0)), spec, spec],
        out_specs=(spec, spec, spec, spec),
        scratch_shapes=[pltpu.VMEM((SEL_GROUP, PEER_TOPK, LANES), F32),
                        pltpu.VMEM((SEL_GROUP, PEER_TOPK, LANES), F32)],
        compiler_params=_params("parallel", "parallel"),
        name="peer_select",
    )(pos, sat, sbt)


def _peer_kernel(hf_ref, u_ref, vt_ref, cnt_ref, rb_ref, ea_ref, eb_ref, x1_ref, gf_ref, y_ref, acc_ref,
                 *, na, tc):
    e = pl.program_id(1)
    nk = PEER_NKEYS
    tm = hf_ref.shape[0]
    chunks = [slice(c * tc, (c + 1) * tc) for c in range(tm // tc)]

    @pl.when(e == 0)
    def _():
        acc_ref[...] = jnp.zeros_like(acc_ref)

    def row(ref, h, a, cs):
        r = jnp.broadcast_to(ref[h, pl.ds(a, 1), cs], (_BF16_ROWS, tc)).astype(BF16)
        return jnp.tile(r, (nk // _BF16_ROWS, 1))

    acts = [_nt(u_ref[...], hf_ref[cs, :]) for cs in chunks]
    zero = jnp.zeros((nk, tc), BF16)
    for cs, act in zip(chunks, acts):
        gel = (act * (1.0 + lax.erf(act))).astype(BF16)
        slabs = []
        for ai in range(na):
            a = e * na + ai
            g = zero
            for h in range(PEER_HEADS):
                taken = rb_ref[h, :, cs] < row(cnt_ref, h, a, cs)
                g = g + jnp.where(taken, eb_ref[h, :, cs], zero) * row(ea_ref, h, a, cs)
            slabs.append(g * gel[ai * nk:(ai + 1) * nk, :])
        acc_ref[:, cs] += _nn(vt_ref[...], jnp.concatenate(slabs, axis=0))

    @pl.when(e == pl.num_programs(1) - 1)
    def _():
        y_ref[...] = x1_ref[...] + gf_ref[...] * acc_ref[...].T


def _peer_experts(hf, u_bf, vt_bf, cnt, rb, ea, eb, x1, gf3, *, tm, eb_sz, tiles_per_mod):
    n, d = hf.shape
    nblk = u_bf.shape[0] // eb_sz
    r = gf3.shape[1]
    na = eb_sz // PEER_NKEYS
    sel = pl.BlockSpec((PEER_HEADS, PEER_NKEYS, tm), lambda t, e: (0, 0, t))
    return pl.pallas_call(
        functools.partial(_peer_kernel, na=na, tc=min(PEER_TC, tm)),
        out_shape=jax.ShapeDtypeStruct((n, d), F32),
        grid=(n // tm, nblk),
        in_specs=[pl.BlockSpec((tm, d), lambda t, e: (t, 0)),
                  pl.BlockSpec((eb_sz, d), lambda t, e: (e, 0)),
                  pl.BlockSpec((d, eb_sz), lambda t, e: (0, e)),
                  sel, sel, sel, sel,
                  pl.BlockSpec((tm, d), lambda t, e: (t, 0)),
                  pl.BlockSpec((None, r, d), lambda t, e: (t // tiles_per_mod, 0, 0))],
        out_specs=pl.BlockSpec((tm, d), lambda t, e: (t, 0)),
        scratch_shapes=[pltpu.VMEM((d, tm), F32)],
        compiler_params=_params("parallel", "arbitrary"),
        name="peer_experts",
    )(hf, u_bf, vt_bf, cnt, rb, ea, eb, x1, gf3)


IN_TM = 256
ML_CHUNK = 256
ML_CHUNK_DEC = 128
SB_DEC_PAGES = 4
SB_DEC_NEW_KEYS = 16
SB_PROMPT_HEADS = 4
SEL_TS = 512
SEL_GROUP = 2
PEER_TM = 1024
PEER_EB = 512
PEER_TC = 512


def _layer_weights(w_in, b_gates, qnorm_g, knorm_g, w_out, w_query, sub_keys_a, sub_keys_b, expert_u,
                   expert_v):
    w_m = ML_HEADS * ML_DH
    ng = 2 * ML_HEADS
    g0 = 4 * w_m
    d = w_in.shape[0]
    w_main = jnp.concatenate([w_in[:, :g0], w_in[:, g0 + ng:]], axis=1).astype(BF16)
    wg = w_in[:, g0:g0 + ng]
    w_gate = jnp.pad(wg, ((0, 0), (0, LANES - ng))).astype(BF16)
    w_gate_t = wg.T.astype(BF16)
    bg_row = jnp.pad(b_gates, (0, LANES - ng)).reshape(1, LANES)
    bg_col = b_gates.reshape(ng, 1)
    head = np.arange(SB_HEADS * SB_DH) // SB_DH
    bd = jnp.asarray(head[:, None] == head[None, :], BF16)
    return dict(w_main=w_main, w_gate=w_gate, w_gate_t=w_gate_t, bg_row=bg_row, bg_col=bg_col,
                gq=qnorm_g.reshape(1, -1), gk=knorm_g.reshape(1, -1), bd=bd,
                w_out=w_out.astype(BF16), w_query=w_query.astype(BF16),
                ka=sub_keys_a.astype(BF16), kb=sub_keys_b.astype(BF16),
                u=(expert_u * 2.0 ** -0.5).astype(BF16), vt=expert_v.T.astype(BF16), d=d)


def kernel(x_prompt, x_sample, cache_sb_k, cache_sb_v, state_mlstm_C, state_mlstm_n, state_mlstm_m,
           page_table, c_prompt, c_sample, w_ada, b_ada, norm_mix_g, norm_ffn_g, w_in, b_gates, qnorm_g,
           knorm_g, sb_bias, mlstm_head_g, w_out, w_query, sub_keys_a, sub_keys_b, expert_u, expert_v):
    depth = w_ada.shape[0]
    assert depth == 1, "single-layer trunk"
    bp, tp, d = x_prompt.shape
    bs, ts, _ = x_sample.shape
    w_m = ML_HEADS * ML_DH
    w_s = SB_HEADS * SB_DH
    psz = cache_sb_k.shape[2]

    lw = _layer_weights(w_in[0], b_gates[0], qnorm_g[0], knorm_g[0], w_out[0], w_query[0], sub_keys_a[0],
                        sub_keys_b[0], expert_u[0], expert_v[0])
    g_mix = norm_mix_g[0].reshape(1, d)
    g_ffn = norm_ffn_g[0].reshape(1, d)
    gh = mlstm_head_g[0].reshape(1, w_m)
    bias = sb_bias[0].astype(F32)
    pos = jnp.asarray(_cand_tables())

    nrow = bp + bs
    pad = (-nrow) % 8
    c_all = jnp.pad(jnp.concatenate([c_prompt, c_sample], axis=0).astype(F32), ((0, pad), (0, 0)))
    mod = _ada(c_all, w_ada[0].astype(BF16), b_ada[0].reshape(1, -1))
    mods = [mod[:, i * d:(i + 1) * d] for i in range(6)]

    def group_mods(lo, hi, per_token):
        out = []
        for m in mods:
            m = m[lo:hi]
            if per_token:
                m = jnp.repeat(m, ts, axis=0).reshape(1, (hi - lo) * ts, d)
            else:
                m = m.reshape(hi - lo, 1, d)
            out.append(m)
        return out

    def peer(hf, x1, gf3, sat, sbt, n_tok, tm, tpm):
        sel_ts = min(SEL_TS, n_tok)
        cnt, rb, ea, eb = _peer_select(pos, sat, sbt, ts=sel_ts)
        return _peer_experts(hf, lw["u"], lw["vt"], cnt, rb, ea, eb, x1, gf3, tm=tm, eb_sz=PEER_EB,
                             tiles_per_mod=tpm)

    sh_a, sc_a, ga, sh_f, sc_f, gf = group_mods(0, bp, False)
    xp = x_prompt.reshape(bp * tp, d).astype(F32)
    tpm = tp // IN_TM
    (mq, mk, mv, og, gcol, grow, qn, kbf, ks_p, vs_p, vt) = _in_proj(
        xp, sc_a, sh_a, g_mix, lw["w_main"], lw["w_gate"], lw["w_gate_t"], lw["bg_row"], lw["bg_col"],
        lw["gq"], lw["gk"], lw["bd"], tm=IN_TM, tiles_per_mod=tpm)
    zc = jnp.zeros((bp, ML_HEADS, ML_DH, ML_DH), F32)
    zn = jnp.zeros((bp, ML_HEADS, ML_DH), F32)
    hm, c_p, n_p, m_p = _mlstm(mq, mk, mv, og, gcol, grow, zc, zn, zn, gh, nb=bp, chunk=ML_CHUNK)
    hs = _sb_prompt(bias, qn, kbf, vt, nb=bp, blk=IN_TM, heads=SB_PROMPT_HEADS)
    x1, hf, sat, sbt = _out_proj(xp, hm, hs, ga, sc_f, sh_f, g_ffn, lw["w_out"], lw["w_query"], lw["ka"],
                                 lw["kb"], tm=IN_TM, tiles_per_mod=tpm)
    y_p = peer(hf, x1, gf, sat, sbt, bp * tp, PEER_TM, tp // PEER_TM)

    sh_a, sc_a, ga, sh_f, sc_f, gf = group_mods(bp, bp + bs, True)
    ns = bs * ts
    xs = x_sample.reshape(ns, d).astype(F32)
    (mq, mk, mv, og, gcol, grow, qn, kbf, ks_s, vs_s, _) = _in_proj(
        xs, sc_a, sh_a, g_mix, lw["w_main"], lw["w_gate"], lw["w_gate_t"], lw["bg_row"], lw["bg_col"],
        lw["gq"], lw["gk"], lw["bd"], tm=ns, tiles_per_mod=1)

    lc = ML_CHUNK_DEC

    def pad_seq(a, fill=0.0):
        a = a.reshape(bs, ts, a.shape[-1])
        return jnp.pad(a, ((0, 0), (0, lc - ts), (0, 0)), constant_values=fill).reshape(bs * lc, -1)

    gcol_fill = jnp.where(jnp.arange(LANES) < ML_HEADS, -1e30, 0.0).astype(F32)
    gcol_p = gcol.reshape(bs, ts, LANES)
    gcol_p = jnp.concatenate(
        [gcol_p, jnp.broadcast_to(gcol_fill, (bs, lc - ts, LANES))], axis=1).reshape(bs * lc, LANES)
    grow_fill = jnp.where(jnp.arange(8) < ML_HEADS, -1e30, 0.0).astype(F32)
    grow_p = grow.reshape(8, bs, ts)
    grow_p = jnp.concatenate(
        [grow_p, jnp.broadcast_to(grow_fill[:, None, None], (8, bs, lc - ts))], axis=2).reshape(8, bs * lc)
    m0 = jnp.broadcast_to(state_mlstm_m[0].astype(F32)[:, :, None], (bs, ML_HEADS, ML_DH))
    hm_pad, c_s, n_s, m_s = _mlstm(pad_seq(mq), pad_seq(mk), pad_seq(mv), pad_seq(og), gcol_p, grow_p,
                                   state_mlstm_C[0].astype(F32), state_mlstm_n[0].astype(F32), m0, gh,
                                   nb=bs, chunk=lc)
    hm = hm_pad.reshape(bs, lc, w_m)[:, :ts].reshape(ns, w_m)

    nk_new = SB_DEC_NEW_KEYS
    heads_first = lambda a: a.reshape(bs, ts, SB_HEADS, SB_DH).transpose(0, 2, 1, 3)
    q_all = heads_first(qn).reshape(bs, SB_HEADS * ts, SB_DH)
    bias_row = jnp.pad(jnp.repeat(bias * LOG2E, ts), (0, LANES - SB_HEADS * ts)).reshape(1, LANES)
    prow = psz * SB_HEADS
    pad_keys = lambda a: jnp.pad(a.reshape(bs, ts, SB_HEADS, SB_DH), ((0, 0), (0, nk_new - ts), (0, 0), (0, 0))
                                 ).reshape(bs, nk_new * SB_HEADS, SB_DH)
    ck = cache_sb_k[0].reshape(cache_sb_k.shape[1], prow, SB_DH)
    cv = cache_sb_v[0].reshape(cache_sb_v.shape[1], prow, SB_DH)
    acc = _sb_decode(page_table.astype(jnp.int32), q_all, bias_row, pad_keys(kbf),
                     pad_keys(vs_s.astype(BF16)), ck, cv, pages=SB_DEC_PAGES, tq=ts, new_keys=nk_new)
    hs = acc.reshape(bs, SB_DH, SB_HEADS, ts).transpose(0, 3, 2, 1).reshape(ns, w_s).astype(BF16)
    x1, hf, sat, sbt = _out_proj(xs, hm, hs, ga, sc_f, sh_f, g_ffn, lw["w_out"], lw["w_query"], lw["ka"],
                                 lw["kb"], tm=ns, tiles_per_mod=1)
    y_s = peer(hf, x1, gf, sat, sbt, ns, ns, 1)

    dt_k, dt_v = cache_sb_k.dtype, cache_sb_v.dtype
    return (y_p.reshape(bp, tp, d).astype(x_prompt.dtype),
            y_s.reshape(bs, ts, d).astype(x_sample.dtype),
            ks_p.reshape(1, bp, tp, SB_HEADS, SB_DH).astype(dt_k),
            vs_p.reshape(1, bp, tp, SB_HEADS, SB_DH).astype(dt_v),
            c_p[None], n_p[None], m_p[None, :, :, 0],
            ks_s.reshape(1, bs, ts, SB_HEADS, SB_DH).astype(dt_k),
            vs_s.reshape(1, bs, ts, SB_HEADS, SB_DH).astype(dt_v),
            c_s[None], n_s[None], m_s[None, :, :, 0])
```

```python
import functools

import numpy as np
import jax
import jax.numpy as jnp
from jax import lax
from jax.experimental import pallas as pl
from jax.experimental.pallas import tpu as pltpu

F32 = jnp.float32
BF16 = jnp.bfloat16

EPS = 1e-6
GATE_SOFTCAP = 15.0
SB_HEADS = 8
SB_DH = 64
ML_HEADS = 4
ML_DH = 128
PEER_HEADS = 8
PEER_NKEYS = 128
PEER_TOPK = 16

LOG2E = 1.4426950408889634
LANES = 128
_BF16_ROWS = 16
VMEM_LIMIT = 56 * 1024 * 1024

NEG_INF = float("-inf")


def _nn(a, b):
    return jnp.dot(a, b, preferred_element_type=F32)


def _nt(a, b):
    return lax.dot_general(a, b, (((1,), (1,)), ((), ())), preferred_element_type=F32)


def _split(x):
    hi = x.astype(BF16)
    lo = (x - hi.astype(F32)).astype(BF16)
    return hi, lo


def _softplus(z):
    return jnp.maximum(z, 0.0) + jnp.log1p(jnp.exp(-jnp.abs(z)))


def _log_sigmoid(z):
    return -_softplus(-z)


def _softcap(a):
    return GATE_SOFTCAP * jnp.tanh(a / GATE_SOFTCAP)


def _params(*sem):
    return pltpu.CompilerParams(dimension_semantics=sem, vmem_limit_bytes=VMEM_LIMIT)


def _ada_kernel(c_ref, w_ref, b_ref, o_ref):
    c = c_ref[...]
    a = (c * jax.nn.sigmoid(c)).astype(BF16)
    o_ref[...] = _nn(a, w_ref[...]) + b_ref[...]


def _ada(c_all, w_ada, b_ada):
    rows, d = c_all.shape
    n_out = w_ada.shape[1]
    return pl.pallas_call(
        _ada_kernel,
        out_shape=jax.ShapeDtypeStruct((rows, n_out), F32),
        grid=(n_out // d,),
        in_specs=[pl.BlockSpec((rows, d), lambda j: (0, 0)),
                  pl.BlockSpec((d, d), lambda j: (0, j)),
                  pl.BlockSpec((1, d), lambda j: (0, j))],
        out_specs=pl.BlockSpec((rows, d), lambda j: (0, j)),
        compiler_params=_params("parallel"),
        name="ada_mod",
    )(c_all, w_ada, b_ada)


def _in_kernel(x_ref, sc_ref, sh_ref, g_ref, w_ref, wg_ref, wgt_ref, bgr_ref, bgc_ref, gq_ref, gk_ref,
               bd_ref, mq_ref, mk_ref, mv_ref, og_ref, gcol_ref, grow_ref, qn_ref, kbf_ref, ks_ref,
               vs_ref, vt_ref):
    w_m = ML_HEADS * ML_DH
    x = x_ref[...]
    h = x * lax.rsqrt(jnp.mean(x * x, axis=-1, keepdims=True) + EPS) * g_ref[...]
    h = h * (1.0 + sc_ref[...]) + sh_ref[...]
    hb = h.astype(BF16)

    def seg(k):
        return _nn(hb, w_ref[:, k * w_m:(k + 1) * w_m])

    mq_ref[...] = seg(0).astype(BF16)
    mk_ref[...] = (seg(1) * ML_DH ** -0.5).astype(BF16)
    mv_ref[...] = seg(2).astype(BF16)
    og_ref[...] = jax.nn.sigmoid(seg(3)).astype(BF16)

    gc = _softcap(_nn(hb, wg_ref[...]) + bgr_ref[...])
    lane = lax.broadcasted_iota(jnp.int32, gc.shape, 1)
    gcol_ref[...] = jnp.where(lane < ML_HEADS, gc, _log_sigmoid(gc))
    gr = _softcap(_nt(wgt_ref[...], hb) + bgc_ref[...])
    row = lax.broadcasted_iota(jnp.int32, gr.shape, 0)
    grow_ref[...] = jnp.where(row < ML_HEADS, gr, _log_sigmoid(gr))

    def head_norm(s, g):
        ss = _nn((s * s).astype(BF16), bd_ref[...])
        return s * lax.rsqrt(ss * (1.0 / SB_DH) + EPS) * g

    qn_ref[...] = (head_norm(seg(4), gq_ref[...]) * (SB_DH ** -0.5 * LOG2E)).astype(BF16)
    kn = head_norm(seg(5), gk_ref[...])
    ks_ref[...] = kn
    kbf_ref[...] = kn.astype(BF16)
    sv = seg(6)
    vs_ref[...] = sv
    nkb, _, kb = vt_ref.shape
    for c in range(nkb):
        for p in range(w_m // LANES):
            vt_ref[c, p * LANES:(p + 1) * LANES, :] = (
                sv[c * kb:(c + 1) * kb, p * LANES:(p + 1) * LANES].T.astype(BF16))


def _in_proj(x2d, sc3, sh3, g_mix, w_main, w_gate, w_gate_t, bg_row, bg_col, gq, gk, bd, *, tm,
             tiles_per_mod, kb):
    n, d = x2d.shape
    nt = n // tm
    r = sc3.shape[1]
    w = ML_HEADS * ML_DH
    const = lambda t: (0, 0)
    mod_map = lambda t: (t // tiles_per_mod, 0, 0)
    tile = lambda t: (t, 0)
    tok = lambda dt: jax.ShapeDtypeStruct((n, w), dt)
    return pl.pallas_call(
        _in_kernel,
        out_shape=(tok(BF16), tok(BF16), tok(BF16), tok(BF16),
                   jax.ShapeDtypeStruct((n, LANES), F32), jax.ShapeDtypeStruct((8, n), F32),
                   tok(BF16), tok(BF16), tok(F32), tok(F32),
                   jax.ShapeDtypeStruct((n // kb, w, kb), BF16)),
        grid=(nt,),
        in_specs=[pl.BlockSpec((tm, d), tile),
                  pl.BlockSpec((None, r, d), mod_map),
                  pl.BlockSpec((None, r, d), mod_map),
                  pl.BlockSpec((1, d), const),
                  pl.BlockSpec(w_main.shape, const),
                  pl.BlockSpec(w_gate.shape, const),
                  pl.BlockSpec(w_gate_t.shape, const),
                  pl.BlockSpec(bg_row.shape, const),
                  pl.BlockSpec(bg_col.shape, const),
                  pl.BlockSpec((1, w), const),
                  pl.BlockSpec((1, w), const),
                  pl.BlockSpec((w, w), const)],
        out_specs=(pl.BlockSpec((tm, w), tile), pl.BlockSpec((tm, w), tile), pl.BlockSpec((tm, w), tile),
                   pl.BlockSpec((tm, w), tile), pl.BlockSpec((tm, LANES), tile),
                   pl.BlockSpec((8, tm), lambda t: (0, t)),
                   pl.BlockSpec((tm, w), tile), pl.BlockSpec((tm, w), tile), pl.BlockSpec((tm, w), tile),
                   pl.BlockSpec((tm, w), tile), pl.BlockSpec((tm // kb, w, kb), lambda t: (t, 0, 0))),
        compiler_params=_params("parallel"),
        name="in_proj",
    )(x2d, sc3, sh3, g_mix, w_main, w_gate, w_gate_t, bg_row, bg_col, gq, gk, bd)


def _mlstm_kernel(q_ref, k_ref, v_ref, og_ref, gcol_ref, grow_ref, c0_ref, n0_ref, m0_ref, gh_ref,
                  hm_ref, c_ref, n_ref, m_ref, *, chunk):
    L = chunk

    @pl.when(pl.program_id(1) == 0)
    def _():
        c_ref[...] = c0_ref[...]
        n_ref[...] = n0_ref[...]
        m_ref[...] = m0_ref[...]

    t_i = lax.broadcasted_iota(jnp.int32, (L, L), 0)
    s_i = lax.broadcasted_iota(jnp.int32, (L, L), 1)
    causal = s_i <= t_i
    tri_lo = causal.astype(BF16)
    tri_up = (t_i <= s_i).astype(BF16)

    gcol = gcol_ref[...]
    grow = grow_ref[...]
    chi, clo = _split(gcol)
    bc = _nn(tri_lo, chi) + _nn(tri_lo, clo)
    rhi, rlo = _split(grow)
    br = _nn(rhi, tri_up) + _nn(rlo, tri_up)

    for h in range(ML_HEADS):
        sl = slice(h * ML_DH, (h + 1) * ML_DH)
        icol = gcol[:, h:h + 1]
        bcol = bc[:, ML_HEADS + h:ML_HEADS + h + 1]
        irow = grow[h:h + 1, :]
        brow = br[ML_HEADS + h:ML_HEADS + h + 1, :]
        m_prev = m_ref[h:h + 1, 0:1]

        dmat = jnp.where(causal, bcol - brow + irow, NEG_INF)
        inter = bcol + m_prev
        mt = jnp.maximum(inter, jnp.max(dmat, axis=1, keepdims=True))
        w_intra = jnp.exp(dmat - mt)
        w_inter = jnp.exp(inter - mt)

        q = q_ref[:, sl]
        k = k_ref[:, sl]
        v = v_ref[:, sl]
        sw = _nt(q, k) * w_intra
        n_row = n_ref[h:h + 1, :]
        num = w_inter * _nt(q, c_ref[h].astype(BF16)) + _nn(sw.astype(BF16), v)
        qn = jnp.sum(q.astype(F32) * n_row, axis=1, keepdims=True)
        dot = w_inter * qn + jnp.sum(sw, axis=1, keepdims=True)
        hh = num / jnp.maximum(jnp.abs(dot), jnp.exp(-mt))
        hn = hh * lax.rsqrt(jnp.mean(hh * hh, axis=1, keepdims=True) + EPS) * gh_ref[:, sl]
        hm_ref[:, sl] = (hn * og_ref[:, sl].astype(F32)).astype(BF16)

        m_last = mt[L - 1:L, :]
        b_last = bcol[L - 1:L, :]
        decay = jnp.exp(b_last - bcol + icol - m_last)
        carry = jnp.exp(b_last + m_prev - m_last)
        vd = (v.astype(F32) * decay).T.astype(BF16)
        c_ref[h] = carry * c_ref[h] + _nn(vd, k)
        n_ref[h:h + 1, :] = carry * n_row + jnp.sum(decay * k.astype(F32), axis=0, keepdims=True)
        m_ref[h:h + 1, :] = jnp.broadcast_to(m_last, (1, ML_DH))


def _mlstm(mq, mk, mv, og, gcol, grow, c0, n0, m0, gh, *, nb, chunk):
    n, w = mq.shape
    nc = n // (nb * chunk)
    blk = lambda b, c: (b * nc + c, 0)
    st4 = lambda b, c: (b, 0, 0, 0)
    st3 = lambda b, c: (b, 0, 0)
    return pl.pallas_call(
        functools.partial(_mlstm_kernel, chunk=chunk),
        out_shape=(jax.ShapeDtypeStruct((n, w), BF16),
                   jax.ShapeDtypeStruct((nb, ML_HEADS, ML_DH, ML_DH), F32),
                   jax.ShapeDtypeStruct((nb, ML_HEADS, ML_DH), F32),
                   jax.ShapeDtypeStruct((nb, ML_HEADS, ML_DH), F32)),
        grid=(nb, nc),
        in_specs=[pl.BlockSpec((chunk, w), blk), pl.BlockSpec((chunk, w), blk),
                  pl.BlockSpec((chunk, w), blk), pl.BlockSpec((chunk, w), blk),
                  pl.BlockSpec((chunk, LANES), blk),
                  pl.BlockSpec((8, chunk), lambda b, c: (0, b * nc + c)),
                  pl.BlockSpec((None, ML_HEADS, ML_DH, ML_DH), st4),
                  pl.BlockSpec((None, ML_HEADS, ML_DH), st3),
                  pl.BlockSpec((None, ML_HEADS, ML_DH), st3),
                  pl.BlockSpec((1, w), lambda b, c: (0, 0))],
        out_specs=(pl.BlockSpec((chunk, w), blk),
                   pl.BlockSpec((None, ML_HEADS, ML_DH, ML_DH), st4),
                   pl.BlockSpec((None, ML_HEADS, ML_DH), st3),
                   pl.BlockSpec((None, ML_HEADS, ML_DH), st3)),
        compiler_params=_params("parallel", "arbitrary"),
        name="mlstm",
    )(mq, mk, mv, og, gcol, grow, c0, n0, m0, gh)


def _sb_kernel(bias_ref, q_ref, k_ref, vt_ref, o_ref, acc_ref, *, blk, heads):
    grp0 = pl.program_id(1)
    qi = pl.program_id(2)
    lane = lax.broadcasted_iota(jnp.int32, (blk, LANES), 1)
    r_i = lax.broadcasted_iota(jnp.int32, (blk, blk), 0)
    c_i = lax.broadcasted_iota(jnp.int32, (blk, blk), 1)
    suffix = (c_i >= r_i).astype(BF16)
    visible = r_i < c_i

    q_neg, bias_neg = [], []
    for hh in range(heads):
        grp, half = divmod(hh, 2)
        q2 = q_ref[:, grp * LANES:(grp + 1) * LANES]
        keep = (lane >= half * SB_DH) & (lane < (half + 1) * SB_DH)
        q_neg.append(jnp.where(keep, -q2, jnp.zeros_like(q2)))
        bias_neg.append(-LOG2E * bias_ref[heads * grp0 + hh])
    acc_ref[...] = jnp.zeros_like(acc_ref)

    def step(js, runs, masked):
        hs = range(heads)
        xs = range(len(js))
        starts = [pl.multiple_of(j * blk, blk) for j in js]
        gl = [slice((hh // 2) * LANES, (hh // 2 + 1) * LANES) for hh in hs]
        zn = [[_nt(k_ref[pl.ds(starts[x], blk), gl[hh]], q_neg[hh]) + bias_neg[hh] for hh in hs]
              for x in xs]
        lm = [[jnp.minimum(z, 0.0) - jnp.log(1.0 + jnp.exp2(-jnp.abs(z))) * LOG2E for z in row]
              for row in zn]
        if masked:
            lm = [[jnp.where(visible, v, 0.0) for v in row] for row in lm]
        p = [[_nn(suffix, v.astype(BF16)) for v in row] for row in lm]
        for x in xs:
            a = [jnp.exp2((runs[hh] + p[x][hh]) - zn[x][hh]) for hh in hs]
            if masked:
                a = [jnp.where(visible, v, 0.0) for v in a]
            for hh in hs:
                acc_ref[hh] += _nn(vt_ref[js[x], gl[hh], :], a[hh].astype(BF16))
            runs = tuple(runs[hh] + p[x][hh][0:1, :] for hh in hs)
        return runs

    runs = step([qi], tuple(jnp.zeros((1, blk), F32) for _ in range(heads)), True)
    odd = qi % 2
    runs = lax.cond(odd == 1, lambda r: step([qi - 1], r, False), lambda r: r, runs)
    base = qi - 1 - odd
    lax.fori_loop(0, (qi - odd) // 2, lambda t, r: step([base - 2 * t, base - 2 * t - 1], r, False), runs)
    for grp in range(heads // 2):
        both = jnp.concatenate([acc_ref[2 * grp, 0:SB_DH, :], acc_ref[2 * grp + 1, SB_DH:, :]], axis=0)
        o_ref[:, grp * LANES:(grp + 1) * LANES] = both.T.astype(BF16)


def _sb_prompt(sb_bias, qn, kbf, vt, *, nb, blk, heads):
    n, w = qn.shape
    t = n // nb
    nq = t // blk
    wb = heads * SB_DH
    return pl.pallas_call(
        functools.partial(_sb_kernel, blk=blk, heads=heads),
        out_shape=jax.ShapeDtypeStruct((n, w), BF16),
        grid=(nb, w // wb, nq),
        in_specs=[pl.BlockSpec(memory_space=pltpu.SMEM),
                  pl.BlockSpec((blk, wb), lambda b, p, i: (b * nq + i, p)),
                  pl.BlockSpec((t, wb), lambda b, p, i: (b, p)),
                  pl.BlockSpec((nq, wb, blk), lambda b, p, i: (b, p, 0))],
        out_specs=pl.BlockSpec((blk, wb), lambda b, p, i: (b * nq + i, p)),
        scratch_shapes=[pltpu.VMEM((heads, LANES, blk), F32)],
        compiler_params=_params("parallel", "parallel", "arbitrary"),
        name="sb_prompt",
    )(sb_bias, qn, kbf, vt)


def _sbd_kernel(pt_ref, q_ref, bias_ref, knew_ref, vnew_ref, *refs, pages, tq, new_keys):
    k_refs = refs[:pages]
    v_refs = refs[pages:2 * pages]
    o_ref = refs[2 * pages]
    acc_ref, run_ref, fold_ref = refs[2 * pages + 1:]
    j = pl.program_id(1)
    nh = SB_HEADS
    cols = nh * tq
    q = q_ref[...]
    bias = bias_ref[...]

    def blocks(kfs, vfs, nkeys, visible):
        rows = nkeys * nh
        xs = range(len(kfs))
        own = [lax.broadcasted_iota(jnp.int32, (nkeys, LANES), 1) // tq == h for h in range(nh)]
        r_i = lax.broadcasted_iota(jnp.int32, (nkeys, nkeys), 0)
        c_i = lax.broadcasted_iota(jnp.int32, (nkeys, nkeys), 1)
        suffix = (c_i >= r_i).astype(BF16)
        for x in xs:
            fold_ref[x, 0:rows, 0:cols] = _nt(kfs[x], q)
        zs, lms = [], []
        for x in xs:
            z = bias
            for h in range(nh):
                z = z + jnp.where(own[h], fold_ref[x, pl.ds(h, nkeys, stride=nh), :], 0.0)
            lm = -(jnp.maximum(z, 0.0) + jnp.log(1.0 + jnp.exp2(-jnp.abs(z))) * LOG2E)
            if visible is not None:
                lm = jnp.where(visible, lm, 0.0)
            zs.append(z)
            lms.append(lm.astype(BF16))
        ps = [_nn(suffix, x) for x in lms]
        run = run_ref[...]
        afs = []
        for x in xs:
            a = jnp.exp2(zs[x] + run + ps[x])
            if visible is not None:
                a = jnp.where(visible, a, 0.0)
            for h in range(nh):
                fold_ref[x, pl.ds(h, nkeys, stride=nh), :] = jnp.where(own[h], a, 0.0)
            afs.append(fold_ref[x, 0:rows, 0:cols].astype(BF16))
            run = run + ps[x][0:1, :]
        run_ref[...] = run
        upd = None
        for x in xs:
            t = lax.dot_general(vfs[x], afs[x], (((0,), (0,)), ((), ())), preferred_element_type=F32)
            upd = t if upd is None else upd + t
        acc_ref[...] += upd

    @pl.when(j == 0)
    def _():
        acc_ref[...] = jnp.zeros_like(acc_ref)
        run_ref[...] = jnp.zeros_like(run_ref)
        fold_ref[...] = jnp.zeros_like(fold_ref)
        k_pos = lax.broadcasted_iota(jnp.int32, (new_keys, LANES), 0)
        q_pos = lax.broadcasted_iota(jnp.int32, (new_keys, LANES), 1) % tq
        blocks([knew_ref[...]], [vnew_ref[...]], new_keys, k_pos < q_pos)

    psz = k_refs[0].shape[0]
    flat = lambda r: r[...].reshape(psz * nh, r.shape[-1]).astype(BF16)
    blocks([flat(r) for r in k_refs], [flat(r) for r in v_refs], psz, None)

    @pl.when(j == pl.num_programs(1) - 1)
    def _():
        o_ref[...] = acc_ref[...]


def _sb_decode(page_table, q_all, bias_row, knew, vnew, cache_k, cache_v, *, pages, tq, new_keys):
    nb, npg = page_table.shape
    psz, nh, dh = cache_k.shape[1:]
    prow = psz * nh
    cols = SB_HEADS * tq
    steps = npg // pages

    def page_map(p):
        return lambda b, j, pt: (pt[b, npg - 1 - (j * pages + p)], 0, 0, 0)

    seq3 = lambda b, j, pt: (b, 0, 0)
    const = lambda b, j, pt: (0, 0)
    grid_spec = pltpu.PrefetchScalarGridSpec(
        num_scalar_prefetch=1,
        grid=(nb, steps),
        in_specs=[pl.BlockSpec((None, cols, dh), seq3),
                  pl.BlockSpec((1, LANES), const),
                  pl.BlockSpec((None, new_keys * SB_HEADS, dh), seq3),
                  pl.BlockSpec((None, new_keys * SB_HEADS, dh), seq3)]
                 + [pl.BlockSpec((None, psz, nh, dh), page_map(p)) for p in range(pages)]
                 + [pl.BlockSpec((None, psz, nh, dh), page_map(p)) for p in range(pages)],
        out_specs=pl.BlockSpec((None, dh, cols), seq3),
        scratch_shapes=[pltpu.VMEM((dh, cols), F32), pltpu.VMEM((1, LANES), F32),
                        pltpu.VMEM((pages, prow, LANES), F32)],
    )
    return pl.pallas_call(
        functools.partial(_sbd_kernel, pages=pages, tq=tq, new_keys=new_keys),
        out_shape=jax.ShapeDtypeStruct((nb, dh, cols), F32),
        grid_spec=grid_spec,
        compiler_params=_params("parallel", "arbitrary"),
        name="sb_decode",
    )(page_table, q_all, bias_row, knew, vnew, *([cache_k] * pages), *([cache_v] * pages))


def _out_kernel(x_ref, hm_ref, hs_ref, ga_ref, scf_ref, shf_ref, gf_ref, wo_ref, wq_ref, ka_ref, kb_ref,
                x1_ref, hf_ref, sat_ref, sbt_ref):
    w_m = hm_ref.shape[1]
    mix = _nn(hm_ref[...], wo_ref[0:w_m, :]) + _nn(hs_ref[...], wo_ref[w_m:, :])
    x1 = x_ref[...] + ga_ref[...] * mix
    x1_ref[...] = x1
    hf = x1 * lax.rsqrt(jnp.mean(x1 * x1, axis=-1, keepdims=True) + EPS) * gf_ref[...]
    hf = (hf * (1.0 + scf_ref[...]) + shf_ref[...]).astype(BF16)
    hf_ref[...] = hf
    dk = ka_ref.shape[1]
    for h in range(PEER_HEADS):
        qa = _nn(hf, wq_ref[:, 2 * h * dk:(2 * h + 1) * dk]).astype(BF16)
        qb = _nn(hf, wq_ref[:, (2 * h + 1) * dk:(2 * h + 2) * dk]).astype(BF16)
        sat_ref[h] = _nt(ka_ref[...], qa)
        sbt_ref[h] = _nt(kb_ref[...], qb)


def _out_proj(x2d, hm, hs, ga3, scf3, shf3, g_ffn, w_out, w_query, ka, kb, *, tm, tiles_per_mod):
    n, d = x2d.shape
    nt = n // tm
    r = ga3.shape[1]
    w = hm.shape[1]
    const = lambda t: (0, 0)
    mod_map = lambda t: (t // tiles_per_mod, 0, 0)
    tile = lambda t: (t, 0)
    sc_shape = jax.ShapeDtypeStruct((PEER_HEADS, PEER_NKEYS, n), F32)
    sc_spec = pl.BlockSpec((PEER_HEADS, PEER_NKEYS, tm), lambda t: (0, 0, t))
    return pl.pallas_call(
        _out_kernel,
        out_shape=(jax.ShapeDtypeStruct((n, d), F32), jax.ShapeDtypeStruct((n, d), BF16),
                   sc_shape, sc_shape),
        grid=(nt,),
        in_specs=[pl.BlockSpec((tm, d), tile), pl.BlockSpec((tm, w), tile), pl.BlockSpec((tm, w), tile),
                  pl.BlockSpec((None, r, d), mod_map), pl.BlockSpec((None, r, d), mod_map),
                  pl.BlockSpec((None, r, d), mod_map),
                  pl.BlockSpec((1, d), const), pl.BlockSpec(w_out.shape, const),
                  pl.BlockSpec(w_query.shape, const), pl.BlockSpec(ka.shape, const),
                  pl.BlockSpec(kb.shape, const)],
        out_specs=(pl.BlockSpec((tm, d), tile), pl.BlockSpec((tm, d), tile), sc_spec, sc_spec),
        compiler_params=_params("parallel"),
        name="out_proj",
    )(x2d, hm, hs, ga3, scf3, shf3, g_ffn, w_out, w_query, ka, kb)


_CAND_ROWS = 80
_TAKEN = 2.0 ** 100


def _cand_tables():
    pos = np.full((_CAND_ROWS, LANES), 1e9, np.float32)
    for j in range(PEER_TOPK):
        pos[j] = j
    for i in range(1, 8):
        for j in range(PEER_TOPK // (i + 1)):
            pos[16 + 8 * (i - 1) + j] = i * PEER_TOPK + j
    for r in range(8):
        pos[72 + r] = (8 + r) * PEER_TOPK
    return pos


def _sel_kernel(pos_ref, sa_ref, sb_ref, cnt_ref, rb_ref, ea_ref, eb_ref, va_ref, vb_ref, *, sub):
    nk = PEER_NKEYS
    iota = lax.broadcasted_iota(jnp.int32, (nk, LANES), 0).astype(F32)
    pos = pos_ref[...]
    valid = pos < 1e8

    def pick(rem, r, v_ref):
        m = jnp.max(rem, axis=0, keepdims=True)
        idx = jnp.min(jnp.where(rem == m, iota, float(nk)), axis=0, keepdims=True)
        v_ref[pl.ds(r, 1), :] = m
        return jnp.where(iota == idx, -_TAKEN * (1.0 + jnp.asarray(r, F32) * (1.0 / 64)), rem)

    def rank_of(rem):
        return jnp.where(rem <= -_TAKEN, (rem * (-1.0 / _TAKEN) - 1.0) * 64.0, float(PEER_TOPK))

    grp = min(SEL_GROUP, sub)
    for s0 in range(0, sub, grp):
        lss = [slice((s0 + g) * LANES, (s0 + g + 1) * LANES) for g in range(grp)]
        sas = [sa_ref[:, ls] for ls in lss]
        sbs = [sb_ref[:, ls] for ls in lss]

        def body1(r, c):
            out = []
            for g in range(grp):
                out.append(pick(c[2 * g], r, va_ref.at[g]))
                out.append(pick(c[2 * g + 1], r, vb_ref.at[g]))
            return tuple(out)

        rems = lax.fori_loop(0, PEER_TOPK, body1, tuple(x for g in range(grp) for x in (sas[g], sbs[g])))
        vas = [va_ref[g] for g in range(grp)]
        vbs = [vb_ref[g] for g in range(grp)]
        cands, smaxs = [], []
        for g in range(grp):
            va, vb = vas[g], vbs[g]
            blocks = [va[0:1, :] + vb]
            for i in range(1, 8):
                blocks.append(va[i:i + 1, :] + vb[0:8, :])
            blocks.append(va[8:16, :] + vb[0:1, :])
            cands.append(jnp.where(valid, jnp.concatenate(blocks, axis=0), NEG_INF))
            smaxs.append(va[0:1, :] + vb[0:1, :])

        def body2(r, carry):
            out = []
            for g in range(grp):
                cand, taken, z = carry[3 * g:3 * g + 3]
                m = jnp.max(cand, axis=0, keepdims=True)
                p = jnp.min(jnp.where(cand == m, pos, 1e9), axis=0, keepdims=True)
                sel = pos == p
                out += [jnp.where(sel, NEG_INF, cand), jnp.where(sel, 1.0, taken), z + jnp.exp(m - smaxs[g])]
            return tuple(out)

        init = []
        for g in range(grp):
            init += [cands[g], jnp.zeros_like(cands[g]), jnp.zeros_like(smaxs[g])]
        fin = lax.fori_loop(0, PEER_TOPK, body2, tuple(init))

        for g in range(grp):
            ls, sa, sb, va, vb = lss[g], sas[g], sbs[g], vas[g], vbs[g]
            taken, z = fin[3 * g + 1], fin[3 * g + 2]
            rank_a = rank_of(rems[2 * g])
            rank_b = rank_of(rems[2 * g + 1])
            cnt = jnp.zeros_like(sa)
            for i in range(PEER_TOPK):
                if i == 0:
                    c = jnp.sum(taken[0:16, :], axis=0, keepdims=True)
                elif i < 8:
                    c = jnp.sum(taken[16 + 8 * (i - 1):16 + 8 * i, :], axis=0, keepdims=True)
                else:
                    c = taken[72 + (i - 8):73 + (i - 8), :]
                cnt = jnp.where(rank_a == float(i), c, cnt)
            cnt_ref[:, ls] = cnt
            rb_ref[:, ls] = rank_b.astype(BF16)
            ea_ref[:, ls] = jnp.exp(sa - va[0:1, :]) * (2.0 ** -0.5 / z)
            eb_ref[:, ls] = jnp.exp(sb - vb[0:1, :]).astype(BF16)


def _peer_select(pos, sat, sbt, *, ts):
    nh, nk, n = sat.shape
    spec = pl.BlockSpec((None, nk, ts), lambda t, h: (h, 0, t))
    shp = jax.ShapeDtypeStruct((nh, nk, n), F32)
    shp_bf = jax.ShapeDtypeStruct((nh, nk, n), BF16)
    return pl.pallas_call(
        functools.partial(_sel_kernel, sub=ts // LANES),
        out_shape=(shp, shp_bf, shp, shp_bf),
        grid=(n // ts, nh),
        in_specs=[pl.BlockSpec(pos.shape, lambda t, h: (0, 0)), spec, spec],
        out_specs=(spec, spec, spec, spec),
        scratch_shapes=[pltpu.VMEM((SEL_GROUP, PEER_TOPK, LANES), F32),
                        pltpu.VMEM((SEL_GROUP, PEER_TOPK, LANES), F32)],
        compiler_params=_params("parallel", "parallel"),
        name="peer_select",
    )(pos, sat, sbt)


def _peer_kernel(hf_ref, u_ref, vt_ref, cnt_ref, rb_ref, ea_ref, eb_ref, x1_ref, gf_ref, y_ref, acc_ref,
                 *, na, tc):
    e = pl.program_id(1)
    nk = PEER_NKEYS
    tm = hf_ref.shape[0]
    chunks = [slice(c * tc, (c + 1) * tc) for c in range(tm // tc)]

    @pl.when(e == 0)
    def _():
        acc_ref[...] = jnp.zeros_like(acc_ref)

    def row(ref, h, a, cs):
        r = jnp.broadcast_to(ref[h, pl.ds(a, 1), cs], (_BF16_ROWS, tc)).astype(BF16)
        return jnp.tile(r, (nk // _BF16_ROWS, 1))

    acts = [_nt(u_ref[...], hf_ref[cs, :]) for cs in chunks]
    zero = jnp.zeros((nk, tc), BF16)
    for cs, act in zip(chunks, acts):
        gel = (act * (1.0 + lax.erf(act))).astype(BF16)
        slabs = []
        for ai in range(na):
            a = e * na + ai
            g = zero
            for h in range(PEER_HEADS):
                taken = rb_ref[h, :, cs] < row(cnt_ref, h, a, cs)
                g = g + jnp.where(taken, eb_ref[h, :, cs], zero) * row(ea_ref, h, a, cs)
            slabs.append(g * gel[ai * nk:(ai + 1) * nk, :])
        acc_ref[:, cs] += _nn(vt_ref[...], jnp.concatenate(slabs, axis=0))

    @pl.when(e == pl.num_programs(1) - 1)
    def _():
        y_ref[...] = x1_ref[...] + gf_ref[...] * acc_ref[...].T


def _peer_experts(hf, u_bf, vt_bf, cnt, rb, ea, eb, x1, gf3, *, tm, eb_sz, tiles_per_mod):
    n, d = hf.shape
    nblk = u_bf.shape[0] // eb_sz
    r = gf3.shape[1]
    na = eb_sz // PEER_NKEYS
    sel = pl.BlockSpec((PEER_HEADS, PEER_NKEYS, tm), lambda t, e: (0, 0, t))
    return pl.pallas_call(
        functools.partial(_peer_kernel, na=na, tc=min(PEER_TC, tm)),
        out_shape=jax.ShapeDtypeStruct((n, d), F32),
        grid=(n // tm, nblk),
        in_specs=[pl.BlockSpec((tm, d), lambda t, e: (t, 0)),
                  pl.BlockSpec((eb_sz, d), lambda t, e: (e, 0)),
                  pl.BlockSpec((d, eb_sz), lambda t, e: (0, e)),
                  sel, sel, sel, sel,
                  pl.BlockSpec((tm, d), lambda t, e: (t, 0)),
                  pl.BlockSpec((None, r, d), lambda t, e: (t // tiles_per_mod, 0, 0))],
        out_specs=pl.BlockSpec((tm, d), lambda t, e: (t, 0)),
        scratch_shapes=[pltpu.VMEM((d, tm), F32)],
        compiler_params=_params("parallel", "arbitrary"),
        name="peer_experts",
    )(hf, u_bf, vt_bf, cnt, rb, ea, eb, x1, gf3)


PROJ_TM = 512
SB_BLK = 256
ML_CHUNK = 256
ML_CHUNK_DEC = 128
SB_DEC_PAGES = 4
SB_DEC_NEW_KEYS = 16
SB_PROMPT_HEADS = 4
SEL_TS = 512
SEL_GROUP = 2
PEER_TM = 1024
PEER_EB = 512
PEER_TC = 512


def _layer_weights(w_in, b_gates, qnorm_g, knorm_g, w_out, w_query, sub_keys_a, sub_keys_b, expert_u,
                   expert_v):
    w_m = ML_HEADS * ML_DH
    ng = 2 * ML_HEADS
    g0 = 4 * w_m
    d = w_in.shape[0]
    w_main = jnp.concatenate([w_in[:, :g0], w_in[:, g0 + ng:]], axis=1).astype(BF16)
    wg = w_in[:, g0:g0 + ng]
    w_gate = jnp.pad(wg, ((0, 0), (0, LANES - ng))).astype(BF16)
    w_gate_t = wg.T.astype(BF16)
    bg_row = jnp.pad(b_gates, (0, LANES - ng)).reshape(1, LANES)
    bg_col = b_gates.reshape(ng, 1)
    head = np.arange(SB_HEADS * SB_DH) // SB_DH
    bd = jnp.asarray(head[:, None] == head[None, :], BF16)
    return dict(w_main=w_main, w_gate=w_gate, w_gate_t=w_gate_t, bg_row=bg_row, bg_col=bg_col,
                gq=qnorm_g.reshape(1, -1), gk=knorm_g.reshape(1, -1), bd=bd,
                w_out=w_out.astype(BF16), w_query=w_query.astype(BF16),
                ka=sub_keys_a.astype(BF16), kb=sub_keys_b.astype(BF16),
                u=(expert_u * 2.0 ** -0.5).astype(BF16), vt=expert_v.T.astype(BF16), d=d)


def kernel(x_prompt, x_sample, cache_sb_k, cache_sb_v, state_mlstm_C, state_mlstm_n, state_mlstm_m,
           page_table, c_prompt, c_sample, w_ada, b_ada, norm_mix_g, norm_ffn_g, w_in, b_gates, qnorm_g,
           knorm_g, sb_bias, mlstm_head_g, w_out, w_query, sub_keys_a, sub_keys_b, expert_u, expert_v):
    depth = w_ada.shape[0]
    assert depth == 1, "single-layer trunk"
    bp, tp, d = x_prompt.shape
    bs, ts, _ = x_sample.shape
    w_m = ML_HEADS * ML_DH
    w_s = SB_HEADS * SB_DH
    psz = cache_sb_k.shape[2]

    lw = _layer_weights(w_in[0], b_gates[0], qnorm_g[0], knorm_g[0], w_out[0], w_query[0], sub_keys_a[0],
                        sub_keys_b[0], expert_u[0], expert_v[0])
    g_mix = norm_mix_g[0].reshape(1, d)
    g_ffn = norm_ffn_g[0].reshape(1, d)
    gh = mlstm_head_g[0].reshape(1, w_m)
    bias = sb_bias[0].astype(F32)
    pos = jnp.asarray(_cand_tables())

    nrow = bp + bs
    pad = (-nrow) % 8
    c_all = jnp.pad(jnp.concatenate([c_prompt, c_sample], axis=0).astype(F32), ((0, pad), (0, 0)))
    mod = _ada(c_all, w_ada[0].astype(BF16), b_ada[0].reshape(1, -1))
    mods = [mod[:, i * d:(i + 1) * d] for i in range(6)]

    def group_mods(lo, hi, per_token):
        out = []
        for m in mods:
            m = m[lo:hi]
            if per_token:
                m = jnp.repeat(m, ts, axis=0).reshape(1, (hi - lo) * ts, d)
            else:
                m = m.reshape(hi - lo, 1, d)
            out.append(m)
        return out

    def peer(hf, x1, gf3, sat, sbt, n_tok, tm, tpm):
        sel_ts = min(SEL_TS, n_tok)
        cnt, rb, ea, eb = _peer_select(pos, sat, sbt, ts=sel_ts)
        return _peer_experts(hf, lw["u"], lw["vt"], cnt, rb, ea, eb, x1, gf3, tm=tm, eb_sz=PEER_EB,
                             tiles_per_mod=tpm)

    sh_a, sc_a, ga, sh_f, sc_f, gf = group_mods(0, bp, False)
    xp = x_prompt.reshape(bp * tp, d).astype(F32)
    tpm = tp // PROJ_TM
    (mq, mk, mv, og, gcol, grow, qn, kbf, ks_p, vs_p, vt) = _in_proj(
        xp, sc_a, sh_a, g_mix, lw["w_main"], lw["w_gate"], lw["w_gate_t"], lw["bg_row"], lw["bg_col"],
        lw["gq"], lw["gk"], lw["bd"], tm=PROJ_TM, tiles_per_mod=tpm, kb=SB_BLK)
    zc = jnp.zeros((bp, ML_HEADS, ML_DH, ML_DH), F32)
    zn = jnp.zeros((bp, ML_HEADS, ML_DH), F32)
    hm, c_p, n_p, m_p = _mlstm(mq, mk, mv, og, gcol, grow, zc, zn, zn, gh, nb=bp, chunk=ML_CHUNK)
    hs = _sb_prompt(bias, qn, kbf, vt, nb=bp, blk=SB_BLK, heads=SB_PROMPT_HEADS)
    x1, hf, sat, sbt = _out_proj(xp, hm, hs, ga, sc_f, sh_f, g_ffn, lw["w_out"], lw["w_query"], lw["ka"],
                                 lw["kb"], tm=PROJ_TM, tiles_per_mod=tpm)
    y_p = peer(hf, x1, gf, sat, sbt, bp * tp, PEER_TM, tp // PEER_TM)

    sh_a, sc_a, ga, sh_f, sc_f, gf = group_mods(bp, bp + bs, True)
    ns = bs * ts
    xs = x_sample.reshape(ns, d).astype(F32)
    (mq, mk, mv, og, gcol, grow, qn, kbf, ks_s, vs_s, _) = _in_proj(
        xs, sc_a, sh_a, g_mix, lw["w_main"], lw["w_gate"], lw["w_gate_t"], lw["bg_row"], lw["bg_col"],
        lw["gq"], lw["gk"], lw["bd"], tm=ns, tiles_per_mod=1, kb=ns)

    lc = ML_CHUNK_DEC

    def pad_seq(a, fill=0.0):
        a = a.reshape(bs, ts, a.shape[-1])
        return jnp.pad(a, ((0, 0), (0, lc - ts), (0, 0)), constant_values=fill).reshape(bs * lc, -1)

    gcol_fill = jnp.where(jnp.arange(LANES) < ML_HEADS, -1e30, 0.0).astype(F32)
    gcol_p = gcol.reshape(bs, ts, LANES)
    gcol_p = jnp.concatenate(
        [gcol_p, jnp.broadcast_to(gcol_fill, (bs, lc - ts, LANES))], axis=1).reshape(bs * lc, LANES)
    grow_fill = jnp.where(jnp.arange(8) < ML_HEADS, -1e30, 0.0).astype(F32)
    grow_p = grow.reshape(8, bs, ts)
    grow_p = jnp.concatenate(
        [grow_p, jnp.broadcast_to(grow_fill[:, None, None], (8, bs, lc - ts))], axis=2).reshape(8, bs * lc)
    m0 = jnp.broadcast_to(state_mlstm_m[0].astype(F32)[:, :, None], (bs, ML_HEADS, ML_DH))
    hm_pad, c_s, n_s, m_s = _mlstm(pad_seq(mq), pad_seq(mk), pad_seq(mv), pad_seq(og), gcol_p, grow_p,
                                   state_mlstm_C[0].astype(F32), state_mlstm_n[0].astype(F32), m0, gh,
                                   nb=bs, chunk=lc)
    hm = hm_pad.reshape(bs, lc, w_m)[:, :ts].reshape(ns, w_m)

    nk_new = SB_DEC_NEW_KEYS
    heads_first = lambda a: a.reshape(bs, ts, SB_HEADS, SB_DH).transpose(0, 2, 1, 3)
    q_all = heads_first(qn).reshape(bs, SB_HEADS * ts, SB_DH)
    bias_row = jnp.pad(jnp.repeat(bias * LOG2E, ts), (0, LANES - SB_HEADS * ts)).reshape(1, LANES)
    pad_keys = lambda a: jnp.pad(a.reshape(bs, ts, SB_HEADS, SB_DH), ((0, 0), (0, nk_new - ts), (0, 0), (0, 0))
                                 ).reshape(bs, nk_new * SB_HEADS, SB_DH)
    acc = _sb_decode(page_table.astype(jnp.int32), q_all, bias_row, pad_keys(kbf),
                     pad_keys(vs_s.astype(BF16)), cache_sb_k[0], cache_sb_v[0], pages=SB_DEC_PAGES, tq=ts,
                     new_keys=nk_new)
    hs = acc.reshape(bs, SB_DH, SB_HEADS, ts).transpose(0, 3, 2, 1).reshape(ns, w_s).astype(BF16)
    x1, hf, sat, sbt = _out_proj(xs, hm, hs, ga, sc_f, sh_f, g_ffn, lw["w_out"], lw["w_query"], lw["ka"],
                                 lw["kb"], tm=ns, tiles_per_mod=1)
    y_s = peer(hf, x1, gf, sat, sbt, ns, ns, 1)

    dt_k, dt_v = cache_sb_k.dtype, cache_sb_v.dtype
    return (y_p.reshape(bp, tp, d).astype(x_prompt.dtype),
            y_s.reshape(bs, ts, d).astype(x_sample.dtype),
            ks_p.reshape(1, bp, tp, SB_HEADS, SB_DH).astype(dt_k),
            vs_p.reshape(1, bp, tp, SB_HEADS, SB_DH).astype(dt_v),
            c_p[None], n_p[None], m_p[None, :, :, 0],
            ks_s.reshape(1, bs, ts, SB_HEADS, SB_DH).astype(dt_k),
            vs_s.reshape(1, bs, ts, SB_HEADS, SB_DH).astype(dt_v),
            c_s[None], n_s[None], m_s[None, :, :, 0])
```

```python
import functools

import numpy as np
import jax
import jax.numpy as jnp
from jax import lax
from jax.experimental import pallas as pl
from jax.experimental.pallas import tpu as pltpu

F32 = jnp.float32
BF16 = jnp.bfloat16

EPS = 1e-6
GATE_SOFTCAP = 15.0
SB_HEADS = 8
SB_DH = 64
ML_HEADS = 4
ML_DH = 128
PEER_HEADS = 8
PEER_NKEYS = 128
PEER_TOPK = 16

LOG2E = 1.4426950408889634
LANES = 128
_BF16_ROWS = 16
VMEM_LIMIT = 56 * 1024 * 1024

NEG_INF = float("-inf")


def _nn(a, b):
    return jnp.dot(a, b, preferred_element_type=F32)


def _nt(a, b):
    return lax.dot_general(a, b, (((1,), (1,)), ((), ())), preferred_element_type=F32)


def _split(x):
    hi = x.astype(BF16)
    lo = (x - hi.astype(F32)).astype(BF16)
    return hi, lo


def _softplus(z):
    return jnp.maximum(z, 0.0) + jnp.log1p(jnp.exp(-jnp.abs(z)))


def _log_sigmoid(z):
    return -_softplus(-z)


def _softcap(a):
    return GATE_SOFTCAP * jnp.tanh(a / GATE_SOFTCAP)


def _params(*sem):
    return pltpu.CompilerParams(dimension_semantics=sem, vmem_limit_bytes=VMEM_LIMIT)


def _ada_kernel(c_ref, w_ref, b_ref, o_ref):
    c = c_ref[...]
    a = (c * jax.nn.sigmoid(c)).astype(BF16)
    o_ref[...] = _nn(a, w_ref[...]) + b_ref[...]


def _ada(c_all, w_ada, b_ada):
    rows, d = c_all.shape
    n_out = w_ada.shape[1]
    return pl.pallas_call(
        _ada_kernel,
        out_shape=jax.ShapeDtypeStruct((rows, n_out), F32),
        grid=(n_out // d,),
        in_specs=[pl.BlockSpec((rows, d), lambda j: (0, 0)),
                  pl.BlockSpec((d, d), lambda j: (0, j)),
                  pl.BlockSpec((1, d), lambda j: (0, j))],
        out_specs=pl.BlockSpec((rows, d), lambda j: (0, j)),
        compiler_params=_params("parallel"),
        name="ada_mod",
    )(c_all, w_ada, b_ada)


def _in_kernel(x_ref, sc_ref, sh_ref, g_ref, w_ref, wg_ref, wgt_ref, bgr_ref, bgc_ref, gq_ref, gk_ref,
               bd_ref, mq_ref, mk_ref, mv_ref, og_ref, gcol_ref, grow_ref, qn_ref, kbf_ref, ks_ref,
               vs_ref, vt_ref):
    w_m = ML_HEADS * ML_DH
    x = x_ref[...]
    h = x * lax.rsqrt(jnp.mean(x * x, axis=-1, keepdims=True) + EPS) * g_ref[...]
    h = h * (1.0 + sc_ref[...]) + sh_ref[...]
    hb = h.astype(BF16)

    def seg(k):
        return _nn(hb, w_ref[:, k * w_m:(k + 1) * w_m])

    mq_ref[...] = seg(0).astype(BF16)
    mk_ref[...] = (seg(1) * ML_DH ** -0.5).astype(BF16)
    mv_ref[...] = seg(2).astype(BF16)
    og_ref[...] = jax.nn.sigmoid(seg(3)).astype(BF16)

    gc = _softcap(_nn(hb, wg_ref[...]) + bgr_ref[...])
    lane = lax.broadcasted_iota(jnp.int32, gc.shape, 1)
    gcol_ref[...] = jnp.where(lane < ML_HEADS, gc, _log_sigmoid(gc))
    gr = _softcap(_nt(wgt_ref[...], hb) + bgc_ref[...])
    row = lax.broadcasted_iota(jnp.int32, gr.shape, 0)
    grow_ref[...] = jnp.where(row < ML_HEADS, gr, _log_sigmoid(gr))

    def head_norm(s, g):
        ss = _nn((s * s).astype(BF16), bd_ref[...])
        return s * lax.rsqrt(ss * (1.0 / SB_DH) + EPS) * g

    qn_ref[...] = (head_norm(seg(4), gq_ref[...]) * (SB_DH ** -0.5 * LOG2E)).astype(BF16)
    kn = head_norm(seg(5), gk_ref[...])
    ks_ref[...] = kn
    kbf_ref[...] = kn.astype(BF16)
    sv = seg(6)
    vs_ref[...] = sv
    nkb, _, kb = vt_ref.shape
    for c in range(nkb):
        for p in range(w_m // LANES):
            vt_ref[c, p * LANES:(p + 1) * LANES, :] = (
                sv[c * kb:(c + 1) * kb, p * LANES:(p + 1) * LANES].T.astype(BF16))


def _in_proj(x2d, sc3, sh3, g_mix, w_main, w_gate, w_gate_t, bg_row, bg_col, gq, gk, bd, *, tm,
             tiles_per_mod, kb):
    n, d = x2d.shape
    nt = n // tm
    r = sc3.shape[1]
    w = ML_HEADS * ML_DH
    const = lambda t: (0, 0)
    mod_map = lambda t: (t // tiles_per_mod, 0, 0)
    tile = lambda t: (t, 0)
    tok = lambda dt: jax.ShapeDtypeStruct((n, w), dt)
    return pl.pallas_call(
        _in_kernel,
        out_shape=(tok(BF16), tok(BF16), tok(BF16), tok(BF16),
                   jax.ShapeDtypeStruct((n, LANES), F32), jax.ShapeDtypeStruct((8, n), F32),
                   tok(BF16), tok(BF16), tok(F32), tok(F32),
                   jax.ShapeDtypeStruct((n // kb, w, kb), BF16)),
        grid=(nt,),
        in_specs=[pl.BlockSpec((tm, d), tile),
                  pl.BlockSpec((None, r, d), mod_map),
                  pl.BlockSpec((None, r, d), mod_map),
                  pl.BlockSpec((1, d), const),
                  pl.BlockSpec(w_main.shape, const),
                  pl.BlockSpec(w_gate.shape, const),
                  pl.BlockSpec(w_gate_t.shape, const),
                  pl.BlockSpec(bg_row.shape, const),
                  pl.BlockSpec(bg_col.shape, const),
                  pl.BlockSpec((1, w), const),
                  pl.BlockSpec((1, w), const),
                  pl.BlockSpec((w, w), const)],
        out_specs=(pl.BlockSpec((tm, w), tile), pl.BlockSpec((tm, w), tile), pl.BlockSpec((tm, w), tile),
                   pl.BlockSpec((tm, w), tile), pl.BlockSpec((tm, LANES), tile),
                   pl.BlockSpec((8, tm), lambda t: (0, t)),
                   pl.BlockSpec((tm, w), tile), pl.BlockSpec((tm, w), tile), pl.BlockSpec((tm, w), tile),
                   pl.BlockSpec((tm, w), tile), pl.BlockSpec((tm // kb, w, kb), lambda t: (t, 0, 0))),
        compiler_params=_params("parallel"),
        name="in_proj",
    )(x2d, sc3, sh3, g_mix, w_main, w_gate, w_gate_t, bg_row, bg_col, gq, gk, bd)


def _mlstm_kernel(q_ref, k_ref, v_ref, og_ref, gcol_ref, grow_ref, c0_ref, n0_ref, m0_ref, gh_ref,
                  hm_ref, c_ref, n_ref, m_ref, *, chunk):
    L = chunk

    @pl.when(pl.program_id(1) == 0)
    def _():
        c_ref[...] = c0_ref[...]
        n_ref[...] = n0_ref[...]
        m_ref[...] = m0_ref[...]

    t_i = lax.broadcasted_iota(jnp.int32, (L, L), 0)
    s_i = lax.broadcasted_iota(jnp.int32, (L, L), 1)
    causal = s_i <= t_i
    tri_lo = causal.astype(BF16)
    tri_up = (t_i <= s_i).astype(BF16)

    gcol = gcol_ref[...]
    grow = grow_ref[...]
    chi, clo = _split(gcol)
    bc = _nn(tri_lo, chi) + _nn(tri_lo, clo)
    rhi, rlo = _split(grow)
    br = _nn(rhi, tri_up) + _nn(rlo, tri_up)

    for h in range(ML_HEADS):
        sl = slice(h * ML_DH, (h + 1) * ML_DH)
        icol = gcol[:, h:h + 1]
        bcol = bc[:, ML_HEADS + h:ML_HEADS + h + 1]
        irow = grow[h:h + 1, :]
        brow = br[ML_HEADS + h:ML_HEADS + h + 1, :]
        m_prev = m_ref[h:h + 1, 0:1]

        dmat = jnp.where(causal, bcol - brow + irow, NEG_INF)
        inter = bcol + m_prev
        mt = jnp.maximum(inter, jnp.max(dmat, axis=1, keepdims=True))
        w_intra = jnp.exp(dmat - mt)
        w_inter = jnp.exp(inter - mt)

        q = q_ref[:, sl]
        k = k_ref[:, sl]
        v = v_ref[:, sl]
        sw = _nt(q, k) * w_intra
        n_row = n_ref[h:h + 1, :]
        num = w_inter * _nt(q, c_ref[h].astype(BF16)) + _nn(sw.astype(BF16), v)
        qn = jnp.sum(q.astype(F32) * n_row, axis=1, keepdims=True)
        dot = w_inter * qn + jnp.sum(sw, axis=1, keepdims=True)
        hh = num / jnp.maximum(jnp.abs(dot), jnp.exp(-mt))
        hn = hh * lax.rsqrt(jnp.mean(hh * hh, axis=1, keepdims=True) + EPS) * gh_ref[:, sl]
        hm_ref[:, sl] = (hn * og_ref[:, sl].astype(F32)).astype(BF16)

        m_last = mt[L - 1:L, :]
        b_last = bcol[L - 1:L, :]
        decay = jnp.exp(b_last - bcol + icol - m_last)
        carry = jnp.exp(b_last + m_prev - m_last)
        vd = (v.astype(F32) * decay).T.astype(BF16)
        c_ref[h] = carry * c_ref[h] + _nn(vd, k)
        n_ref[h:h + 1, :] = carry * n_row + jnp.sum(decay * k.astype(F32), axis=0, keepdims=True)
        m_ref[h:h + 1, :] = jnp.broadcast_to(m_last, (1, ML_DH))


def _mlstm(mq, mk, mv, og, gcol, grow, c0, n0, m0, gh, *, nb, chunk):
    n, w = mq.shape
    nc = n // (nb * chunk)
    blk = lambda b, c: (b * nc + c, 0)
    st4 = lambda b, c: (b, 0, 0, 0)
    st3 = lambda b, c: (b, 0, 0)
    return pl.pallas_call(
        functools.partial(_mlstm_kernel, chunk=chunk),
        out_shape=(jax.ShapeDtypeStruct((n, w), BF16),
                   jax.ShapeDtypeStruct((nb, ML_HEADS, ML_DH, ML_DH), F32),
                   jax.ShapeDtypeStruct((nb, ML_HEADS, ML_DH), F32),
                   jax.ShapeDtypeStruct((nb, ML_HEADS, ML_DH), F32)),
        grid=(nb, nc),
        in_specs=[pl.BlockSpec((chunk, w), blk), pl.BlockSpec((chunk, w), blk),
                  pl.BlockSpec((chunk, w), blk), pl.BlockSpec((chunk, w), blk),
                  pl.BlockSpec((chunk, LANES), blk),
                  pl.BlockSpec((8, chunk), lambda b, c: (0, b * nc + c)),
                  pl.BlockSpec((None, ML_HEADS, ML_DH, ML_DH), st4),
                  pl.BlockSpec((None, ML_HEADS, ML_DH), st3),
                  pl.BlockSpec((None, ML_HEADS, ML_DH), st3),
                  pl.BlockSpec((1, w), lambda b, c: (0, 0))],
        out_specs=(pl.BlockSpec((chunk, w), blk),
                   pl.BlockSpec((None, ML_HEADS, ML_DH, ML_DH), st4),
                   pl.BlockSpec((None, ML_HEADS, ML_DH), st3),
                   pl.BlockSpec((None, ML_HEADS, ML_DH), st3)),
        compiler_params=_params("parallel", "arbitrary"),
        name="mlstm",
    )(mq, mk, mv, og, gcol, grow, c0, n0, m0, gh)


def _sb_kernel(bias_ref, q_ref, k_ref, vt_ref, o_ref, acc_ref, *, blk, heads):
    grp0 = pl.program_id(1)
    qi = pl.program_id(2)
    lane = lax.broadcasted_iota(jnp.int32, (blk, LANES), 1)
    r_i = lax.broadcasted_iota(jnp.int32, (blk, blk), 0)
    c_i = lax.broadcasted_iota(jnp.int32, (blk, blk), 1)
    suffix = (c_i >= r_i).astype(BF16)
    visible = r_i < c_i

    q_neg, bias_neg = [], []
    for hh in range(heads):
        grp, half = divmod(hh, 2)
        q2 = q_ref[:, grp * LANES:(grp + 1) * LANES]
        keep = (lane >= half * SB_DH) & (lane < (half + 1) * SB_DH)
        q_neg.append(jnp.where(keep, -q2, jnp.zeros_like(q2)))
        bias_neg.append(-LOG2E * bias_ref[heads * grp0 + hh])
    acc_ref[...] = jnp.zeros_like(acc_ref)

    def step(js, runs, masked):
        hs = range(heads)
        xs = range(len(js))
        starts = [pl.multiple_of(j * blk, blk) for j in js]
        gl = [slice((hh // 2) * LANES, (hh // 2 + 1) * LANES) for hh in hs]
        zn = [[_nt(k_ref[pl.ds(starts[x], blk), gl[hh]], q_neg[hh]) + bias_neg[hh] for hh in hs]
              for x in xs]
        lm = [[jnp.minimum(z, 0.0) - jnp.log(1.0 + jnp.exp2(-jnp.abs(z))) * LOG2E for z in row]
              for row in zn]
        if masked:
            lm = [[jnp.where(visible, v, 0.0) for v in row] for row in lm]
        p = [[_nn(suffix, v.astype(BF16)) for v in row] for row in lm]
        for x in xs:
            a = [jnp.exp2((runs[hh] + p[x][hh]) - zn[x][hh]) for hh in hs]
            if masked:
                a = [jnp.where(visible, v, 0.0) for v in a]
            for hh in hs:
                acc_ref[hh] += _nn(vt_ref[js[x], gl[hh], :], a[hh].astype(BF16))
            runs = tuple(runs[hh] + p[x][hh][0:1, :] for hh in hs)
        return runs

    runs = step([qi], tuple(jnp.zeros((1, blk), F32) for _ in range(heads)), True)
    odd = qi % 2
    runs = lax.cond(odd == 1, lambda r: step([qi - 1], r, False), lambda r: r, runs)
    base = qi - 1 - odd
    lax.fori_loop(0, (qi - odd) // 2, lambda t, r: step([base - 2 * t, base - 2 * t - 1], r, False), runs)
    for grp in range(heads // 2):
        both = jnp.concatenate([acc_ref[2 * grp, 0:SB_DH, :], acc_ref[2 * grp + 1, SB_DH:, :]], axis=0)
        o_ref[:, grp * LANES:(grp + 1) * LANES] = both.T.astype(BF16)


def _sb_prompt(sb_bias, qn, kbf, vt, *, nb, blk, heads):
    n, w = qn.shape
    t = n // nb
    nq = t // blk
    wb = heads * SB_DH
    return pl.pallas_call(
        functools.partial(_sb_kernel, blk=blk, heads=heads),
        out_shape=jax.ShapeDtypeStruct((n, w), BF16),
        grid=(nb, w // wb, nq),
        in_specs=[pl.BlockSpec(memory_space=pltpu.SMEM),
                  pl.BlockSpec((blk, wb), lambda b, p, i: (b * nq + i, p)),
                  pl.BlockSpec((t, wb), lambda b, p, i: (b, p)),
                  pl.BlockSpec((nq, wb, blk), lambda b, p, i: (b, p, 0))],
        out_specs=pl.BlockSpec((blk, wb), lambda b, p, i: (b * nq + i, p)),
        scratch_shapes=[pltpu.VMEM((heads, LANES, blk), F32)],
        compiler_params=_params("parallel", "parallel", "arbitrary"),
        name="sb_prompt",
    )(sb_bias, qn, kbf, vt)


def _sbd_kernel(pt_ref, qbd_ref, bias_ref, knew_ref, vnew_ref, *refs, pages, tq):
    k_refs = refs[:pages]
    v_refs = refs[pages:2 * pages]
    o_ref = refs[2 * pages]
    acc_ref, run_ref = refs[2 * pages + 1:]
    j = pl.program_id(1)
    rows = SB_HEADS * tq
    psz = knew_ref.shape[0]
    qbd = qbd_ref[...]
    bias = bias_ref[...]
    r_i = lax.broadcasted_iota(jnp.int32, (psz, psz), 0)
    c_i = lax.broadcasted_iota(jnp.int32, (psz, psz), 1)
    suffix = (r_i >= c_i).astype(BF16)

    def blocks(kps, vps, visible):
        zs = [_nt(qbd, kp) + bias for kp in kps]
        lms = []
        for z in zs:
            lm = -(jnp.maximum(z, 0.0) + jnp.log(1.0 + jnp.exp2(-jnp.abs(z))) * LOG2E)
            if visible is not None:
                lm = jnp.where(visible, lm, 0.0)
            lms.append(lm.astype(BF16))
        ps = [_nn(x, suffix) for x in lms]
        run = run_ref[...]
        upd = None
        for z, p, vp in zip(zs, ps, vps):
            a = jnp.exp2(z + run + p)
            if visible is not None:
                a = jnp.where(visible, a, 0.0)
            t = _nn(a.astype(BF16), vp)
            upd = t if upd is None else upd + t
            run = run + jnp.broadcast_to(p[:, 0:1], run.shape)
        acc_ref[...] += upd
        run_ref[...] = run

    @pl.when(j == 0)
    def _():
        acc_ref[...] = jnp.zeros_like(acc_ref)
        run_ref[...] = jnp.zeros_like(run_ref)
        q_pos = lax.broadcasted_iota(jnp.int32, (rows, psz), 0) % tq
        k_pos = lax.broadcasted_iota(jnp.int32, (rows, psz), 1)
        blocks([knew_ref[...]], [vnew_ref[...]], k_pos < q_pos)

    blocks([r[...] for r in k_refs], [r[...] for r in v_refs], None)

    @pl.when(j == pl.num_programs(1) - 1)
    def _():
        acc = acc_ref[...]
        lane = lax.broadcasted_iota(jnp.int32, (tq, acc.shape[1]), 1)
        out = jnp.zeros((tq, acc.shape[1]), F32)
        for h in range(SB_HEADS):
            keep = (lane >= h * SB_DH) & (lane < (h + 1) * SB_DH)
            out = out + jnp.where(keep, acc[h * tq:(h + 1) * tq, :], 0.0)
        o_ref[...] = out.astype(BF16)


def _sb_decode(page_table, qbd, bias_rep, knew, vnew, cache_k, cache_v, *, pages, tq):
    nb, npg = page_table.shape
    psz, w = cache_k.shape[1:]
    rows = SB_HEADS * tq
    steps = npg // pages

    def page_map(p):
        return lambda b, j, pt: (pt[b, npg - 1 - (j * pages + p)], 0, 0)

    seq3 = lambda b, j, pt: (b, 0, 0)
    grid_spec = pltpu.PrefetchScalarGridSpec(
        num_scalar_prefetch=1,
        grid=(nb, steps),
        in_specs=[pl.BlockSpec((None, rows, w), seq3),
                  pl.BlockSpec((rows, psz), lambda b, j, pt: (0, 0)),
                  pl.BlockSpec((None, psz, w), seq3),
                  pl.BlockSpec((None, psz, w), seq3)]
                 + [pl.BlockSpec((None, psz, w), page_map(p)) for p in range(pages)]
                 + [pl.BlockSpec((None, psz, w), page_map(p)) for p in range(pages)],
        out_specs=pl.BlockSpec((None, tq, w), seq3),
        scratch_shapes=[pltpu.VMEM((rows, w), F32), pltpu.VMEM((rows, psz), F32)],
    )
    return pl.pallas_call(
        functools.partial(_sbd_kernel, pages=pages, tq=tq),
        out_shape=jax.ShapeDtypeStruct((nb, tq, w), BF16),
        grid_spec=grid_spec,
        compiler_params=_params("parallel", "arbitrary"),
        name="sb_decode",
    )(page_table, qbd, bias_rep, knew, vnew, *([cache_k] * pages), *([cache_v] * pages))


def _out_kernel(x_ref, hm_ref, hs_ref, ga_ref, scf_ref, shf_ref, gf_ref, wo_ref, wq_ref, ka_ref, kb_ref,
                x1_ref, hf_ref, sat_ref, sbt_ref):
    w_m = hm_ref.shape[1]
    mix = _nn(hm_ref[...], wo_ref[0:w_m, :]) + _nn(hs_ref[...], wo_ref[w_m:, :])
    x1 = x_ref[...] + ga_ref[...] * mix
    x1_ref[...] = x1
    hf = x1 * lax.rsqrt(jnp.mean(x1 * x1, axis=-1, keepdims=True) + EPS) * gf_ref[...]
    hf = (hf * (1.0 + scf_ref[...]) + shf_ref[...]).astype(BF16)
    hf_ref[...] = hf
    dk = ka_ref.shape[1]
    for h in range(PEER_HEADS):
        qa = _nn(hf, wq_ref[:, 2 * h * dk:(2 * h + 1) * dk]).astype(BF16)
        qb = _nn(hf, wq_ref[:, (2 * h + 1) * dk:(2 * h + 2) * dk]).astype(BF16)
        sat_ref[h] = _nt(ka_ref[...], qa)
        sbt_ref[h] = _nt(kb_ref[...], qb)


def _out_proj(x2d, hm, hs, ga3, scf3, shf3, g_ffn, w_out, w_query, ka, kb, *, tm, tiles_per_mod):
    n, d = x2d.shape
    nt = n // tm
    r = ga3.shape[1]
    w = hm.shape[1]
    const = lambda t: (0, 0)
    mod_map = lambda t: (t // tiles_per_mod, 0, 0)
    tile = lambda t: (t, 0)
    sc_shape = jax.ShapeDtypeStruct((PEER_HEADS, PEER_NKEYS, n), F32)
    sc_spec = pl.BlockSpec((PEER_HEADS, PEER_NKEYS, tm), lambda t: (0, 0, t))
    return pl.pallas_call(
        _out_kernel,
        out_shape=(jax.ShapeDtypeStruct((n, d), F32), jax.ShapeDtypeStruct((n, d), BF16),
                   sc_shape, sc_shape),
        grid=(nt,),
        in_specs=[pl.BlockSpec((tm, d), tile), pl.BlockSpec((tm, w), tile), pl.BlockSpec((tm, w), tile),
                  pl.BlockSpec((None, r, d), mod_map), pl.BlockSpec((None, r, d), mod_map),
                  pl.BlockSpec((None, r, d), mod_map),
                  pl.BlockSpec((1, d), const), pl.BlockSpec(w_out.shape, const),
                  pl.BlockSpec(w_query.shape, const), pl.BlockSpec(ka.shape, const),
                  pl.BlockSpec(kb.shape, const)],
        out_specs=(pl.BlockSpec((tm, d), tile), pl.BlockSpec((tm, d), tile), sc_spec, sc_spec),
        compiler_params=_params("parallel"),
        name="out_proj",
    )(x2d, hm, hs, ga3, scf3, shf3, g_ffn, w_out, w_query, ka, kb)


_CAND_ROWS = 80
_TAKEN = 2.0 ** 100


def _cand_tables():
    pos = np.full((_CAND_ROWS, LANES), 1e9, np.float32)
    for j in range(PEER_TOPK):
        pos[j] = j
    for i in range(1, 8):
        for j in range(PEER_TOPK // (i + 1)):
            pos[16 + 8 * (i - 1) + j] = i * PEER_TOPK + j
    for r in range(8):
        pos[72 + r] = (8 + r) * PEER_TOPK
    return pos


def _sel_kernel(pos_ref, sa_ref, sb_ref, cnt_ref, rb_ref, ea_ref, eb_ref, va_ref, vb_ref, *, sub):
    nk = PEER_NKEYS
    iota = lax.broadcasted_iota(jnp.int32, (nk, LANES), 0).astype(F32)
    pos = pos_ref[...]
    valid = pos < 1e8

    def pick(rem, r, v_ref):
        m = jnp.max(rem, axis=0, keepdims=True)
        idx = jnp.min(jnp.where(rem == m, iota, float(nk)), axis=0, keepdims=True)
        v_ref[pl.ds(r, 1), :] = m
        return jnp.where(iota == idx, -_TAKEN * (1.0 + jnp.asarray(r, F32) * (1.0 / 64)), rem)

    def rank_of(rem):
        return jnp.where(rem <= -_TAKEN, (rem * (-1.0 / _TAKEN) - 1.0) * 64.0, float(PEER_TOPK))

    grp = min(SEL_GROUP, sub)
    for s0 in range(0, sub, grp):
        lss = [slice((s0 + g) * LANES, (s0 + g + 1) * LANES) for g in range(grp)]
        sas = [sa_ref[:, ls] for ls in lss]
        sbs = [sb_ref[:, ls] for ls in lss]

        def body1(r, c):
            out = []
            for g in range(grp):
                out.append(pick(c[2 * g], r, va_ref.at[g]))
                out.append(pick(c[2 * g + 1], r, vb_ref.at[g]))
            return tuple(out)

        rems = lax.fori_loop(0, PEER_TOPK, body1, tuple(x for g in range(grp) for x in (sas[g], sbs[g])))
        vas = [va_ref[g] for g in range(grp)]
        vbs = [vb_ref[g] for g in range(grp)]
        cands, smaxs = [], []
        for g in range(grp):
            va, vb = vas[g], vbs[g]
            blocks = [va[0:1, :] + vb]
            for i in range(1, 8):
                blocks.append(va[i:i + 1, :] + vb[0:8, :])
            blocks.append(va[8:16, :] + vb[0:1, :])
            cands.append(jnp.where(valid, jnp.concatenate(blocks, axis=0), NEG_INF))
            smaxs.append(va[0:1, :] + vb[0:1, :])

        def body2(r, carry):
            out = []
            for g in range(grp):
                cand, taken, z = carry[3 * g:3 * g + 3]
                m = jnp.max(cand, axis=0, keepdims=True)
                p = jnp.min(jnp.where(cand == m, pos, 1e9), axis=0, keepdims=True)
                sel = pos == p
                out += [jnp.where(sel, NEG_INF, cand), jnp.where(sel, 1.0, taken), z + jnp.exp(m - smaxs[g])]
            return tuple(out)

        init = []
        for g in range(grp):
            init += [cands[g], jnp.zeros_like(cands[g]), jnp.zeros_like(smaxs[g])]
        fin = lax.fori_loop(0, PEER_TOPK, body2, tuple(init))

        for g in range(grp):
            ls, sa, sb, va, vb = lss[g], sas[g], sbs[g], vas[g], vbs[g]
            taken, z = fin[3 * g + 1], fin[3 * g + 2]
            rank_a = rank_of(rems[2 * g])
            rank_b = rank_of(rems[2 * g + 1])
            cnt = jnp.zeros_like(sa)
            for i in range(PEER_TOPK):
                if i == 0:
                    c = jnp.sum(taken[0:16, :], axis=0, keepdims=True)
                elif i < 8:
                    c = jnp.sum(taken[16 + 8 * (i - 1):16 + 8 * i, :], axis=0, keepdims=True)
                else:
                    c = taken[72 + (i - 8):73 + (i - 8), :]
                cnt = jnp.where(rank_a == float(i), c, cnt)
            cnt_ref[:, ls] = cnt
            rb_ref[:, ls] = rank_b.astype(BF16)
            ea_ref[:, ls] = jnp.exp(sa - va[0:1, :]) * (2.0 ** -0.5 / z)
            eb_ref[:, ls] = jnp.exp(sb - vb[0:1, :]).astype(BF16)


def _peer_select(pos, sat, sbt, *, ts):
    nh, nk, n = sat.shape
    spec = pl.BlockSpec((None, nk, ts), lambda t, h: (h, 0, t))
    shp = jax.ShapeDtypeStruct((nh, nk, n), F32)
    shp_bf = jax.ShapeDtypeStruct((nh, nk, n), BF16)
    return pl.pallas_call(
        functools.partial(_sel_kernel, sub=ts // LANES),
        out_shape=(shp, shp_bf, shp, shp_bf),
        grid=(n // ts, nh),
        in_specs=[pl.BlockSpec(pos.shape, lambda t, h: (0, 0)), spec, spec],
        out_specs=(spec, spec, spec, spec),
        scratch_shapes=[pltpu.VMEM((SEL_GROUP, PEER_TOPK, LANES), F32),
                        pltpu.VMEM((SEL_GROUP, PEER_TOPK, LANES), F32)],
        compiler_params=_params("parallel", "parallel"),
        name="peer_select",
    )(pos, sat, sbt)


def _peer_kernel(hf_ref, u_ref, vt_ref, cnt_ref, rb_ref, ea_ref, eb_ref, x1_ref, gf_ref, y_ref, acc_ref,
                 *, na, tc):
    e = pl.program_id(1)
    nk = PEER_NKEYS
    tm = hf_ref.shape[0]
    chunks = [slice(c * tc, (c + 1) * tc) for c in range(tm // tc)]

    @pl.when(e == 0)
    def _():
        acc_ref[...] = jnp.zeros_like(acc_ref)

    def row(ref, h, a, cs):
        r = jnp.broadcast_to(ref[h, pl.ds(a, 1), cs], (_BF16_ROWS, tc)).astype(BF16)
        return jnp.tile(r, (nk // _BF16_ROWS, 1))

    acts = [_nt(u_ref[...], hf_ref[cs, :]) for cs in chunks]
    zero = jnp.zeros((nk, tc), BF16)
    for cs, act in zip(chunks, acts):
        gel = (act * (1.0 + lax.erf(act))).astype(BF16)
        slabs = []
        for ai in range(na):
            a = e * na + ai
            g = zero
            for h in range(PEER_HEADS):
                taken = rb_ref[h, :, cs] < row(cnt_ref, h, a, cs)
                g = g + jnp.where(taken, eb_ref[h, :, cs], zero) * row(ea_ref, h, a, cs)
            slabs.append(g * gel[ai * nk:(ai + 1) * nk, :])
        acc_ref[:, cs] += _nn(vt_ref[...], jnp.concatenate(slabs, axis=0))

    @pl.when(e == pl.num_programs(1) - 1)
    def _():
        y_ref[...] = x1_ref[...] + gf_ref[...] * acc_ref[...].T


def _peer_experts(hf, u_bf, vt_bf, cnt, rb, ea, eb, x1, gf3, *, tm, eb_sz, tiles_per_mod):
    n, d = hf.shape
    nblk = u_bf.shape[0] // eb_sz
    r = gf3.shape[1]
    na = eb_sz // PEER_NKEYS
    sel = pl.BlockSpec((PEER_HEADS, PEER_NKEYS, tm), lambda t, e: (0, 0, t))
    return pl.pallas_call(
        functools.partial(_peer_kernel, na=na, tc=min(PEER_TC, tm)),
        out_shape=jax.ShapeDtypeStruct((n, d), F32),
        grid=(n // tm, nblk),
        in_specs=[pl.BlockSpec((tm, d), lambda t, e: (t, 0)),
                  pl.BlockSpec((eb_sz, d), lambda t, e: (e, 0)),
                  pl.BlockSpec((d, eb_sz), lambda t, e: (0, e)),
                  sel, sel, sel, sel,
                  pl.BlockSpec((tm, d), lambda t, e: (t, 0)),
                  pl.BlockSpec((None, r, d), lambda t, e: (t // tiles_per_mod, 0, 0))],
        out_specs=pl.BlockSpec((tm, d), lambda t, e: (t, 0)),
        scratch_shapes=[pltpu.VMEM((d, tm), F32)],
        compiler_params=_params("parallel", "arbitrary"),
        name="peer_experts",
    )(hf, u_bf, vt_bf, cnt, rb, ea, eb, x1, gf3)


PROJ_TM = 512
SB_BLK = 256
ML_CHUNK = 256
ML_CHUNK_DEC = 128
SB_DEC_PAGES = 8
SB_PROMPT_HEADS = 4
SEL_TS = 512
SEL_GROUP = 2
PEER_TM = 1024
PEER_EB = 512
PEER_TC = 512


def _layer_weights(w_in, b_gates, qnorm_g, knorm_g, w_out, w_query, sub_keys_a, sub_keys_b, expert_u,
                   expert_v):
    w_m = ML_HEADS * ML_DH
    ng = 2 * ML_HEADS
    g0 = 4 * w_m
    d = w_in.shape[0]
    w_main = jnp.concatenate([w_in[:, :g0], w_in[:, g0 + ng:]], axis=1).astype(BF16)
    wg = w_in[:, g0:g0 + ng]
    w_gate = jnp.pad(wg, ((0, 0), (0, LANES - ng))).astype(BF16)
    w_gate_t = wg.T.astype(BF16)
    bg_row = jnp.pad(b_gates, (0, LANES - ng)).reshape(1, LANES)
    bg_col = b_gates.reshape(ng, 1)
    head = np.arange(SB_HEADS * SB_DH) // SB_DH
    bd = jnp.asarray(head[:, None] == head[None, :], BF16)
    return dict(w_main=w_main, w_gate=w_gate, w_gate_t=w_gate_t, bg_row=bg_row, bg_col=bg_col,
                gq=qnorm_g.reshape(1, -1), gk=knorm_g.reshape(1, -1), bd=bd,
                w_out=w_out.astype(BF16), w_query=w_query.astype(BF16),
                ka=sub_keys_a.astype(BF16), kb=sub_keys_b.astype(BF16),
                u=(expert_u * 2.0 ** -0.5).astype(BF16), vt=expert_v.T.astype(BF16), d=d)


def kernel(x_prompt, x_sample, cache_sb_k, cache_sb_v, state_mlstm_C, state_mlstm_n, state_mlstm_m,
           page_table, c_prompt, c_sample, w_ada, b_ada, norm_mix_g, norm_ffn_g, w_in, b_gates, qnorm_g,
           knorm_g, sb_bias, mlstm_head_g, w_out, w_query, sub_keys_a, sub_keys_b, expert_u, expert_v):
    depth = w_ada.shape[0]
    assert depth == 1, "single-layer trunk"
    bp, tp, d = x_prompt.shape
    bs, ts, _ = x_sample.shape
    w_m = ML_HEADS * ML_DH
    w_s = SB_HEADS * SB_DH
    psz = cache_sb_k.shape[2]

    lw = _layer_weights(w_in[0], b_gates[0], qnorm_g[0], knorm_g[0], w_out[0], w_query[0], sub_keys_a[0],
                        sub_keys_b[0], expert_u[0], expert_v[0])
    g_mix = norm_mix_g[0].reshape(1, d)
    g_ffn = norm_ffn_g[0].reshape(1, d)
    gh = mlstm_head_g[0].reshape(1, w_m)
    bias = sb_bias[0].astype(F32)
    pos = jnp.asarray(_cand_tables())

    nrow = bp + bs
    pad = (-nrow) % 8
    c_all = jnp.pad(jnp.concatenate([c_prompt, c_sample], axis=0).astype(F32), ((0, pad), (0, 0)))
    mod = _ada(c_all, w_ada[0].astype(BF16), b_ada[0].reshape(1, -1))
    mods = [mod[:, i * d:(i + 1) * d] for i in range(6)]

    def group_mods(lo, hi, per_token):
        out = []
        for m in mods:
            m = m[lo:hi]
            if per_token:
                m = jnp.repeat(m, ts, axis=0).reshape(1, (hi - lo) * ts, d)
            else:
                m = m.reshape(hi - lo, 1, d)
            out.append(m)
        return out

    def peer(hf, x1, gf3, sat, sbt, n_tok, tm, tpm):
        sel_ts = min(SEL_TS, n_tok)
        cnt, rb, ea, eb = _peer_select(pos, sat, sbt, ts=sel_ts)
        return _peer_experts(hf, lw["u"], lw["vt"], cnt, rb, ea, eb, x1, gf3, tm=tm, eb_sz=PEER_EB,
                             tiles_per_mod=tpm)

    sh_a, sc_a, ga, sh_f, sc_f, gf = group_mods(0, bp, False)
    xp = x_prompt.reshape(bp * tp, d).astype(F32)
    tpm = tp // PROJ_TM
    (mq, mk, mv, og, gcol, grow, qn, kbf, ks_p, vs_p, vt) = _in_proj(
        xp, sc_a, sh_a, g_mix, lw["w_main"], lw["w_gate"], lw["w_gate_t"], lw["bg_row"], lw["bg_col"],
        lw["gq"], lw["gk"], lw["bd"], tm=PROJ_TM, tiles_per_mod=tpm, kb=SB_BLK)
    zc = jnp.zeros((bp, ML_HEADS, ML_DH, ML_DH), F32)
    zn = jnp.zeros((bp, ML_HEADS, ML_DH), F32)
    hm, c_p, n_p, m_p = _mlstm(mq, mk, mv, og, gcol, grow, zc, zn, zn, gh, nb=bp, chunk=ML_CHUNK)
    hs = _sb_prompt(bias, qn, kbf, vt, nb=bp, blk=SB_BLK, heads=SB_PROMPT_HEADS)
    x1, hf, sat, sbt = _out_proj(xp, hm, hs, ga, sc_f, sh_f, g_ffn, lw["w_out"], lw["w_query"], lw["ka"],
                                 lw["kb"], tm=PROJ_TM, tiles_per_mod=tpm)
    y_p = peer(hf, x1, gf, sat, sbt, bp * tp, PEER_TM, tp // PEER_TM)

    sh_a, sc_a, ga, sh_f, sc_f, gf = group_mods(bp, bp + bs, True)
    ns = bs * ts
    xs = x_sample.reshape(ns, d).astype(F32)
    (mq, mk, mv, og, gcol, grow, qn, kbf, ks_s, vs_s, _) = _in_proj(
        xs, sc_a, sh_a, g_mix, lw["w_main"], lw["w_gate"], lw["w_gate_t"], lw["bg_row"], lw["bg_col"],
        lw["gq"], lw["gk"], lw["bd"], tm=ns, tiles_per_mod=1, kb=ns)

    lc = ML_CHUNK_DEC

    def pad_seq(a, fill=0.0):
        a = a.reshape(bs, ts, a.shape[-1])
        return jnp.pad(a, ((0, 0), (0, lc - ts), (0, 0)), constant_values=fill).reshape(bs * lc, -1)

    gcol_fill = jnp.where(jnp.arange(LANES) < ML_HEADS, -1e30, 0.0).astype(F32)
    gcol_p = gcol.reshape(bs, ts, LANES)
    gcol_p = jnp.concatenate(
        [gcol_p, jnp.broadcast_to(gcol_fill, (bs, lc - ts, LANES))], axis=1).reshape(bs * lc, LANES)
    grow_fill = jnp.where(jnp.arange(8) < ML_HEADS, -1e30, 0.0).astype(F32)
    grow_p = grow.reshape(8, bs, ts)
    grow_p = jnp.concatenate(
        [grow_p, jnp.broadcast_to(grow_fill[:, None, None], (8, bs, lc - ts))], axis=2).reshape(8, bs * lc)
    m0 = jnp.broadcast_to(state_mlstm_m[0].astype(F32)[:, :, None], (bs, ML_HEADS, ML_DH))
    hm_pad, c_s, n_s, m_s = _mlstm(pad_seq(mq), pad_seq(mk), pad_seq(mv), pad_seq(og), gcol_p, grow_p,
                                   state_mlstm_C[0].astype(F32), state_mlstm_n[0].astype(F32), m0, gh,
                                   nb=bs, chunk=lc)
    hm = hm_pad.reshape(bs, lc, w_m)[:, :ts].reshape(ns, w_m)

    lane_head = jnp.arange(w_s) // SB_DH
    q3 = qn.reshape(bs, 1, ts, w_s)
    qbd = jnp.where(lane_head[None, None, None, :] == jnp.arange(SB_HEADS)[None, :, None, None], q3,
                    jnp.zeros_like(q3)).reshape(bs, SB_HEADS * ts, w_s)
    bias_rep = jnp.broadcast_to(jnp.repeat(bias * LOG2E, ts)[:, None], (SB_HEADS * ts, psz))
    knew = jnp.pad(kbf.reshape(bs, ts, w_s), ((0, 0), (0, psz - ts), (0, 0)))
    vnew = jnp.pad(vs_s.astype(BF16).reshape(bs, ts, w_s), ((0, 0), (0, psz - ts), (0, 0)))
    ck = cache_sb_k[0].astype(BF16).reshape(cache_sb_k.shape[1], psz, w_s)
    cv = cache_sb_v[0].astype(BF16).reshape(cache_sb_v.shape[1], psz, w_s)
    hs = _sb_decode(page_table.astype(jnp.int32), qbd, bias_rep, knew, vnew, ck, cv, pages=SB_DEC_PAGES,
                    tq=ts).reshape(ns, w_s)
    x1, hf, sat, sbt = _out_proj(xs, hm, hs, ga, sc_f, sh_f, g_ffn, lw["w_out"], lw["w_query"], lw["ka"],
                                 lw["kb"], tm=ns, tiles_per_mod=1)
    y_s = peer(hf, x1, gf, sat, sbt, ns, ns, 1)

    dt_k, dt_v = cache_sb_k.dtype, cache_sb_v.dtype
    return (y_p.reshape(bp, tp, d).astype(x_prompt.dtype),
            y_s.reshape(bs, ts, d).astype(x_sample.dtype),
            ks_p.reshape(1, bp, tp, SB_HEADS, SB_DH).astype(dt_k),
            vs_p.reshape(1, bp, tp, SB_HEADS, SB_DH).astype(dt_v),
            c_p[None], n_p[None], m_p[None, :, :, 0],
            ks_s.reshape(1, bs, ts, SB_HEADS, SB_DH).astype(dt_k),
            vs_s.reshape(1, bs, ts, SB_HEADS, SB_DH).astype(dt_v),
            c_s[None], n_s[None], m_s[None, :, :, 0])
```
